```python
import math
import jax, jax.numpy as jnp
from jax import lax
import numpy as np

D_MODEL = 2048
BATCH = 4
SEQ = 8192
DEPTH = 4
DEC_BATCH = 8
DEC_SEQ = 16
PAST_LEN = 2048

CHUNK = 64
N_A = DEPTH // 2
N_B = DEPTH - N_A
HEAD_DIM = 128
MIX_WIDTH = D_MODEL
MEM_HEADS = 4
MEM_LEN = 256
MEM_WIDTH = MEM_HEADS * HEAD_DIM
MAIN_WIDTH = MIX_WIDTH - MEM_WIDTH
GLA_HEADS = 4
GLA_DV = MAIN_WIDTH // GLA_HEADS
GLA_DK = GLA_DV // 2
GLA_KW = GLA_HEADS * GLA_DK
GLA_LOWRANK = 16
GLA_TAU = 16.0
SB_HEADS = MAIN_WIDTH // HEAD_DIM
SB_BLOCK = 128
D_FF = ((8 * D_MODEL // 3 + 255) // 256) * 256
IN_A = 2 * GLA_KW + 2 * MAIN_WIDTH + GLA_LOWRANK + MEM_WIDTH
IN_B = MAIN_WIDTH + MEM_WIDTH
SPLIT_A = (GLA_KW, 2 * GLA_KW, 2 * GLA_KW + MAIN_WIDTH, 2 * GLA_KW + 2 * MAIN_WIDTH,
           2 * GLA_KW + 2 * MAIN_WIDTH + GLA_LOWRANK)
DN_ALPHA = (2 * DEPTH) ** 0.25
DN_BETA = (8 * DEPTH) ** -0.25
LN_EPS = 1e-5
RMS_EPS = 1e-6
SB_SCALE = HEAD_DIM ** -0.5
MEM_SCALE = HEAD_DIM ** -0.5

kernel_name = "yoco_gla_stickbreaking_stream_step"


def layer_norm(x, g, b):
    xf = x.astype(jnp.float32)
    mu = xf.mean(-1, keepdims=True)
    var = jnp.square(xf - mu).mean(-1, keepdims=True)
    return ((xf - mu) * lax.rsqrt(var + LN_EPS)).astype(x.dtype) * g + b


def rms_norm(x, g):
    xf = x.astype(jnp.float32)
    r = lax.rsqrt(jnp.mean(xf * xf, -1, keepdims=True) + RMS_EPS)
    return (xf * r).astype(x.dtype) * g


def gla_scan(q, k, v, log_a, s0):
    B, L, H, DK = q.shape
    DV = v.shape[-1]
    C = CHUNK if L % CHUNK == 0 else L
    n = L // C

    def to_blocks(t):
        return t.reshape(B, n, C, H, t.shape[-1]).transpose(1, 0, 3, 2, 4).astype(jnp.float32)

    qc, kc, vc, ac = to_blocks(q), to_blocks(k), to_blocks(v), to_blocks(log_a)
    causal = jnp.tril(jnp.ones((C, C), bool))[:, :, None]

    def step(S, inp):
        qb, kb, vb, ab = inp
        bcum = jnp.cumsum(ab, axis=2)
        diff = bcum[:, :, :, None, :] - bcum[:, :, None, :, :]
        decay = jnp.exp(jnp.where(causal, diff, -jnp.inf))
        att = jnp.einsum('bhid,bhijd,bhjd->bhij', qb, decay, kb)
        o = (jnp.einsum('bhij,bhjv->bhiv', att, vb)
             + jnp.einsum('bhid,bhdv->bhiv', qb * jnp.exp(bcum), S))
        b_last = bcum[:, :, -1:, :]
        S = (S * jnp.exp(b_last)[:, :, 0, :, None]
             + jnp.einsum('bhjd,bhjv->bhdv', kb * jnp.exp(b_last - bcum), vb))
        return S, o

    S, o = lax.scan(step, s0.astype(jnp.float32), (qc, kc, vc, ac))
    o = o.transpose(1, 0, 3, 2, 4).reshape(B, L, H, DV)
    return o.astype(v.dtype), S.astype(v.dtype)


def gla_mixer(h, w_in, w_lr2, b_gate, g_norm, s0):
    B, L, _ = h.shape
    q, k, v, g, lr, qm = jnp.split(h @ w_in, SPLIT_A, axis=-1)
    q = q.reshape(B, L, GLA_HEADS, GLA_DK) * (GLA_DK ** -0.5)
    k = k.reshape(B, L, GLA_HEADS, GLA_DK)
    v = v.reshape(B, L, GLA_HEADS, GLA_DV)
    log_a = (jax.nn.log_sigmoid((lr @ w_lr2 + b_gate).astype(jnp.float32)) / GLA_TAU)
    log_a = log_a.reshape(B, L, GLA_HEADS, GLA_DK)
    o, S = gla_scan(q, k, v, log_a, s0)
    o = rms_norm(o, g_norm) * jax.nn.silu(g.reshape(B, L, GLA_HEADS, GLA_DV))
    return o.reshape(B, L, MAIN_WIDTH), qm.reshape(B, L, MEM_HEADS, HEAD_DIM), S


def sb_block(q, k, v, q_pos, k_pos):
    z = jnp.einsum('bqhd,bkhd->bhqk', q, k).astype(jnp.float32) * SB_SCALE
    mask = k_pos[None, :] < q_pos[:, None]
    log_beta = jax.nn.log_sigmoid(z)
    log_1mb = jnp.where(mask, log_beta - z, 0.0)
    tail = lax.cumsum(log_1mb, axis=3, reverse=True) - log_1mb
    w = jnp.where(mask, jnp.exp(log_beta + tail), 0.0)
    return jnp.einsum('bhqk,bkhd->bqhd', w, v.astype(jnp.float32)).astype(q.dtype)


def sb_prompt(q, k, v):
    B, L, H, D = q.shape
    nb = L // SB_BLOCK
    qb = q.reshape(B, nb, SB_BLOCK, H, D).transpose(1, 0, 2, 3, 4)
    pos = jnp.arange(L)
    qpos = pos.reshape(nb, SB_BLOCK)
    o = lax.map(lambda a: sb_block(a[0], k, v, a[1], pos), (qb, qpos))
    return o.transpose(1, 0, 2, 3, 4).reshape(B, L, H, D)


def mem_kv(mem, w):
    B, M, _ = mem.shape
    kv = (mem @ w).reshape(B, M, 2, MEM_HEADS, HEAD_DIM)
    return kv[:, :, 0], kv[:, :, 1]


def shared_kv(h, w):
    B, L, _ = h.shape
    kv = (h @ w).reshape(B, L, 2, SB_HEADS, HEAD_DIM)
    return kv[:, :, 0], kv[:, :, 1]


def mem_attend(qm, mk, mv):
    B, L = qm.shape[:2]
    s = jnp.einsum('blhd,bmhd->bhlm', qm, mk).astype(jnp.float32) * MEM_SCALE
    p = jax.nn.softmax(s, axis=-1)
    o = jnp.einsum('bhlm,bmhd->blhd', p, mv.astype(jnp.float32)).astype(qm.dtype)
    return o.reshape(B, L, MEM_WIDTH)


def finish_layer(x, o_main, o_mem, w_o, ln1_g, ln1_b, ln2_g, ln2_b, w_ffn_in, w_ffn_out):
    mix = jnp.concatenate([o_main, o_mem], axis=-1) @ w_o
    x = layer_norm(DN_ALPHA * x + mix, ln1_g, ln1_b)
    gt, up = jnp.split(x @ w_ffn_in, 2, axis=-1)
    return layer_norm(DN_ALPHA * x + (jax.nn.silu(gt) * up) @ w_ffn_out, ln2_g, ln2_b)


def setup_inputs(seed: int = 0) -> dict:
    key = jax.random.key(seed)
    ks = jax.random.split(key, 24)

    def nrm(k, shape, s=1.0):
        return jax.random.normal(k, shape, jnp.float32) * s

    return {
        "x_prompt": nrm(ks[0], (BATCH, SEQ, D_MODEL)),
        "x_sample": nrm(ks[1], (DEC_BATCH, DEC_SEQ, D_MODEL)),
        "mem_prompt": nrm(ks[2], (BATCH, MEM_LEN, D_MODEL)),
        "state_gla": nrm(ks[3], (N_A, DEC_BATCH, GLA_HEADS, GLA_DK, GLA_DV), 0.5),
        "cache_sb_k": nrm(ks[4], (DEC_BATCH, PAST_LEN, SB_HEADS, HEAD_DIM)),
        "cache_sb_v": nrm(ks[5], (DEC_BATCH, PAST_LEN, SB_HEADS, HEAD_DIM)),
        "cache_mem_k": nrm(ks[6], (DEPTH, DEC_BATCH, MEM_LEN, MEM_HEADS, HEAD_DIM)),
        "cache_mem_v": nrm(ks[7], (DEPTH, DEC_BATCH, MEM_LEN, MEM_HEADS, HEAD_DIM)),
        "w_in_a": nrm(ks[8], (N_A, D_MODEL, IN_A), D_MODEL ** -0.5),
        "w_gate_lr": nrm(ks[9], (N_A, GLA_LOWRANK, GLA_KW), GLA_LOWRANK ** -0.5),
        "b_gate": nrm(ks[10], (N_A, GLA_KW), 0.1),
        "gla_norm_g": 1.0 + nrm(ks[11], (N_A, GLA_DV), 0.02),
        "w_in_b": nrm(ks[12], (N_B, D_MODEL, IN_B), D_MODEL ** -0.5),
        "w_kv_shared": nrm(ks[13], (D_MODEL, 2 * MAIN_WIDTH), D_MODEL ** -0.5),
        "w_mem_kv": nrm(ks[14], (DEPTH, D_MODEL, 2 * MEM_WIDTH), D_MODEL ** -0.5),
        "w_o": nrm(ks[15], (DEPTH, MIX_WIDTH, D_MODEL), MIX_WIDTH ** -0.5 * DN_BETA),
        "ln1_g": 1.0 + nrm(ks[16], (DEPTH, D_MODEL), 0.02),
        "ln1_b": nrm(ks[17], (DEPTH, D_MODEL), 0.02),
        "ln2_g": 1.0 + nrm(ks[18], (DEPTH, D_MODEL), 0.02),
        "ln2_b": nrm(ks[19], (DEPTH, D_MODEL), 0.02),
        "w_ffn_in": nrm(ks[20], (DEPTH, D_MODEL, 2 * D_FF), D_MODEL ** -0.5),
        "w_ffn_out": nrm(ks[21], (DEPTH, D_FF, D_MODEL), D_FF ** -0.5 * DN_BETA),
    }


def reference(x_prompt, x_sample, mem_prompt, state_gla, cache_sb_k, cache_sb_v,
              cache_mem_k, cache_mem_v, w_in_a, w_gate_lr, b_gate, gla_norm_g, w_in_b,
              w_kv_shared, w_mem_kv, w_o, ln1_g, ln1_b, ln2_g, ln2_b, w_ffn_in, w_ffn_out):
    xp, xs = x_prompt, x_sample
    Bp, Lp, _ = xp.shape
    Bs, Ls, _ = xs.shape
    s_zero = jnp.zeros((Bp, GLA_HEADS, GLA_DK, GLA_DV), xp.dtype)
    q_pos_s = PAST_LEN + jnp.arange(Ls)
    k_pos_s = jnp.arange(PAST_LEN + Ls)
    gla_p, gla_s, mk_list, mv_list = [], [], [], []
    for l in range(DEPTH):
        mk_p, mv_p = mem_kv(mem_prompt, w_mem_kv[l])
        mk_list.append(mk_p)
        mv_list.append(mv_p)
        if l < N_A:
            op, qmp, sp = gla_mixer(xp, w_in_a[l], w_gate_lr[l], b_gate[l], gla_norm_g[l], s_zero)
            os_, qms, ss = gla_mixer(xs, w_in_a[l], w_gate_lr[l], b_gate[l], gla_norm_g[l], state_gla[l])
            gla_p.append(sp)
            gla_s.append(ss)
        else:
            if l == N_A:
                sb_k_prompt, sb_v_prompt = shared_kv(xp, w_kv_shared)
                sb_k_sample, sb_v_sample = shared_kv(xs, w_kv_shared)
                k_all = jnp.concatenate([cache_sb_k, sb_k_sample], axis=1)
                v_all = jnp.concatenate([cache_sb_v, sb_v_sample], axis=1)
            j = l - N_A
            pp = xp @ w_in_b[j]
            ps = xs @ w_in_b[j]
            qp = pp[..., :MAIN_WIDTH].reshape(Bp, Lp, SB_HEADS, HEAD_DIM)
            qmp = pp[..., MAIN_WIDTH:].reshape(Bp, Lp, MEM_HEADS, HEAD_DIM)
            qs = ps[..., :MAIN_WIDTH].reshape(Bs, Ls, SB_HEADS, HEAD_DIM)
            qms = ps[..., MAIN_WIDTH:].reshape(Bs, Ls, MEM_HEADS, HEAD_DIM)
            op = sb_prompt(qp, sb_k_prompt, sb_v_prompt).reshape(Bp, Lp, MAIN_WIDTH)
            os_ = sb_block(qs, k_all, v_all, q_pos_s, k_pos_s).reshape(Bs, Ls, MAIN_WIDTH)
        omp = mem_attend(qmp, mk_p, mv_p)
        oms = mem_attend(qms, cache_mem_k[l], cache_mem_v[l])
        xp = finish_layer(xp, op, omp, w_o[l], ln1_g[l], ln1_b[l], ln2_g[l], ln2_b[l],
                          w_ffn_in[l], w_ffn_out[l])
        xs = finish_layer(xs, os_, oms, w_o[l], ln1_g[l], ln1_b[l], ln2_g[l], ln2_b[l],
                          w_ffn_in[l], w_ffn_out[l])
    gla_state_prompt = jnp.stack(gla_p, axis=0)
    gla_state_sample = jnp.stack(gla_s, axis=0)
    mem_k_prompt = jnp.stack(mk_list, axis=0)
    mem_v_prompt = jnp.stack(mv_list, axis=0)
    return (xp, xs, gla_state_prompt, gla_state_sample, sb_k_prompt, sb_v_prompt,
            sb_k_sample, sb_v_sample, mem_k_prompt, mem_v_prompt)
```

```python
import functools
import math

import numpy as np
import jax
import jax.numpy as jnp
from jax import lax
from jax.experimental import pallas as pl
from jax.experimental.pallas import tpu as pltpu

D_MODEL = 2048
DEPTH = 4
PAST_LEN = 2048
CHUNK = 64
N_A = DEPTH // 2
HEAD_DIM = 128
MEM_HEADS = 4
MEM_WIDTH = MEM_HEADS * HEAD_DIM
MAIN_WIDTH = D_MODEL - MEM_WIDTH
GLA_HEADS = 4
GLA_DV = MAIN_WIDTH // GLA_HEADS
GLA_DK = GLA_DV // 2
GLA_KW = GLA_HEADS * GLA_DK
GLA_LOWRANK = 16
GLA_TAU = 16.0
SB_HEADS = MAIN_WIDTH // HEAD_DIM
D_FF = ((8 * D_MODEL // 3 + 255) // 256) * 256
DN_ALPHA = (2 * DEPTH) ** 0.25
LN_EPS = 1e-5
RMS_EPS = 1e-6
SB_SCALE = HEAD_DIM ** -0.5
MEM_SCALE = HEAD_DIM ** -0.5

V7X_LANES = 128
V7X_MXU_DIM = 256
V7X_VMEM_BYTES = 64 * 1024 * 1024

GLA_DKP = V7X_MXU_DIM
GLA_LRP = V7X_LANES
A_OFF_V = 0
A_OFF_G = MAIN_WIDTH
A_OFF_Q = 2 * MAIN_WIDTH
A_OFF_K = A_OFF_Q + GLA_HEADS * GLA_DKP
A_OFF_QM = A_OFF_K + GLA_HEADS * GLA_DKP
A_OFF_LR = A_OFF_QM + MEM_WIDTH
A_WIDTH = A_OFF_LR + GLA_LRP

SB_TILE = 256

F32 = jnp.float32
BF16 = jnp.bfloat16


def _vmem_limit(block_bytes, scratch_bytes=0, temp_bytes=0):
    need = 2 * block_bytes + scratch_bytes + temp_bytes + (2 << 20)
    return int(min(max(need, 16 << 20), V7X_VMEM_BYTES - (8 << 20)))


def _nbytes(shape, dtype):
    return int(np.prod(shape)) * jnp.dtype(dtype).itemsize


def _tile(n, pref):
    if n <= pref:
        return n
    t = pref
    while n % t:
        t -= 1
    return t


def _dot(a, b):
    return jnp.dot(a, b, preferred_element_type=F32)


def _dot_nt(a, b):
    return lax.dot_general(a, b, (((1,), (1,)), ((), ())), preferred_element_type=F32)


def _dot_tn(a, b):
    return lax.dot_general(a, b, (((0,), (0,)), ((), ())), preferred_element_type=F32)


def _sigmoid(x):
    return 1.0 / (1.0 + jnp.exp(-x))


def _softplus(x):
    return jnp.maximum(x, 0.0) + jnp.log1p(jnp.exp(-jnp.abs(x)))


def _split_bf16(x):
    hi = x.astype(BF16)
    lo = (x - hi.astype(F32)).astype(BF16)
    return hi, lo


def _mm_kernel(*refs, has_scale):
    if has_scale:
        a_ref, w_ref, s_ref = refs[:3]
        o_refs = refs[3:]
    else:
        a_ref, w_ref = refs[:2]
        o_refs = refs[2:]
    acc = _dot(a_ref[...], w_ref[...])
    if has_scale:
        acc = acc * s_ref[...]
    for o_ref in o_refs:
        o_ref[...] = acc.astype(o_ref.dtype)


def _matmul(a, w, *, out_dtypes, col_scale=None, tm=1024, tn=1024):
    m, k = a.shape
    n = w.shape[1]
    tm, tn = _tile(m, tm), _tile(n, tn)
    in_specs = [pl.BlockSpec((tm, k), lambda j, i: (i, 0)),
                pl.BlockSpec((k, tn), lambda j, i: (0, j))]
    args = [a, w]
    if col_scale is not None:
        in_specs.append(pl.BlockSpec((1, tn), lambda j, i: (0, j)))
        args.append(col_scale)
    blocks = _nbytes((tm, k), BF16) + _nbytes((k, tn), BF16) + sum(_nbytes((tm, tn), d) for d in out_dtypes)
    outs = pl.pallas_call(
        functools.partial(_mm_kernel, has_scale=col_scale is not None),
        grid=(n // tn, m // tm),
        in_specs=in_specs,
        out_specs=[pl.BlockSpec((tm, tn), lambda j, i: (i, j)) for _ in out_dtypes],
        out_shape=[jax.ShapeDtypeStruct((m, n), d) for d in out_dtypes],
        compiler_params=pltpu.CompilerParams(
            dimension_semantics=("arbitrary", "arbitrary"),
            vmem_limit_bytes=_vmem_limit(blocks, temp_bytes=2 * _nbytes((tm, tn), F32))),
        name="proj_matmul",
    )(*args)
    return outs


def _swiglu_kernel(a_ref, wg_ref, wu_ref, o_ref):
    a = a_ref[...]
    gate = _dot(a, wg_ref[...])
    up = _dot(a, wu_ref[...])
    o_ref[...] = (gate * _sigmoid(gate) * up).astype(o_ref.dtype)


def _ffn_in(a, w, *, tm=2048, tn=512):
    m, k = a.shape
    tm, tn = _tile(m, tm), _tile(D_FF, tn)
    nj = D_FF // tn
    blocks = _nbytes((tm, k), BF16) + 2 * _nbytes((k, tn), BF16) + _nbytes((tm, tn), BF16)
    return pl.pallas_call(
        _swiglu_kernel,
        grid=(nj, m // tm),
        in_specs=[pl.BlockSpec((tm, k), lambda j, i: (i, 0)),
                  pl.BlockSpec((k, tn), lambda j, i: (0, j)),
                  pl.BlockSpec((k, tn), lambda j, i: (0, j + nj))],
        out_specs=pl.BlockSpec((tm, tn), lambda j, i: (i, j)),
        out_shape=jax.ShapeDtypeStruct((m, D_FF), BF16),
        compiler_params=pltpu.CompilerParams(
            dimension_semantics=("arbitrary", "arbitrary"),
            vmem_limit_bytes=_vmem_limit(blocks, temp_bytes=4 * _nbytes((tm, tn), F32))),
        name="ffn_in_swiglu",
    )(a, w, w)


def _ln_kernel(*refs, npair, nk):
    aw_refs = refs[:2 * npair]
    x_ref, g_ref, b_ref, of_ref, ob_ref = refs[2 * npair:2 * npair + 5]
    scratch = refs[2 * npair + 5:]

    def finish(acc):
        y = DN_ALPHA * x_ref[...] + acc
        mu = jnp.mean(y, axis=-1, keepdims=True)
        d = y - mu
        var = jnp.mean(d * d, axis=-1, keepdims=True)
        out = d * lax.rsqrt(var + LN_EPS) * g_ref[...] + b_ref[...]
        of_ref[...] = out
        ob_ref[...] = out.astype(ob_ref.dtype)

    part = _dot(aw_refs[0][...], aw_refs[1][...])
    for p in range(1, npair):
        part = part + _dot(aw_refs[2 * p][...], aw_refs[2 * p + 1][...])
    if nk == 1:
        finish(part)
        return
    acc_ref, = scratch
    kk = pl.program_id(1)

    @pl.when(kk == 0)
    def _():
        acc_ref[...] = part

    @pl.when(kk > 0)
    def _():
        acc_ref[...] += part

    @pl.when(kk == nk - 1)
    def _():
        finish(acc_ref[...])


def _matmul_residual_ln(pairs, x, gamma, beta, *, tm=512, tk=None):
    m, n = x.shape
    tm = _tile(m, tm)
    npair = len(pairs)
    nk = 1
    if npair == 1 and tk is not None:
        k = pairs[0][0].shape[1]
        tk = _tile(k, tk)
        nk = k // tk
    in_specs, args, blocks = [], [], 0
    for a, w in pairs:
        ka = tk if nk > 1 else a.shape[1]
        in_specs += [pl.BlockSpec((tm, ka), lambda i, kk: (i, kk)),
                     pl.BlockSpec((ka, n), lambda i, kk: (kk, 0))]
        args += [a, w]
        blocks += _nbytes((tm, ka), BF16) + _nbytes((ka, n), BF16)
    in_specs += [pl.BlockSpec((tm, n), lambda i, kk: (i, 0)),
                 pl.BlockSpec((1, n), lambda i, kk: (0, 0)),
                 pl.BlockSpec((1, n), lambda i, kk: (0, 0))]
    blocks += 2 * _nbytes((tm, n), F32) + _nbytes((tm, n), BF16)
    acc_bytes = _nbytes((tm, n), F32) if nk > 1 else 0
    return pl.pallas_call(
        functools.partial(_ln_kernel, npair=npair, nk=nk),
        grid=(m // tm, nk),
        in_specs=in_specs,
        out_specs=[pl.BlockSpec((tm, n), lambda i, kk: (i, 0)),
                   pl.BlockSpec((tm, n), lambda i, kk: (i, 0))],
        out_shape=[jax.ShapeDtypeStruct((m, n), F32), jax.ShapeDtypeStruct((m, n), BF16)],
        scratch_shapes=[pltpu.VMEM((tm, n), F32)] if nk > 1 else [],
        compiler_params=pltpu.CompilerParams(
            dimension_semantics=("arbitrary", "arbitrary"),
            vmem_limit_bytes=_vmem_limit(blocks, scratch_bytes=acc_bytes,
                                         temp_bytes=3 * _nbytes((tm, n), F32))),
        name="matmul_residual_ln",
    )(*args, x, gamma, beta)


def _gla_tables(c):
    nlev = int(math.log2(c))
    assert 1 << nlev == c
    idx = np.arange(c)
    i, t = idx[:, None], idx[None, :]
    sums, pairs = [], []
    for lev in range(nlev):
        h = c >> (lev + 1)
        p = (i // (2 * h)) * 2 * h + h - 1
        second = (i % (2 * h)) >= h
        sums.append(np.where(second, (t > p) & (t <= i), (t > i) & (t <= p)))
        pairs.append((i // (2 * h) == t // (2 * h)) & second & ((t % (2 * h)) < h))
    sums.append(t <= i)
    sums.append(t > i)
    pairs.append(i == t)
    return (np.concatenate(sums, 0).astype(np.float32), np.stack(pairs, 0).astype(np.float32), nlev)


def _gla_kernel(v_ref, g_ref, q_ref, k_ref, lr_ref, wlr_ref, bg_ref, gn_ref, sum_ref, pair_ref, s0_ref,
                o_ref, sf_ref, st_ref, *, c, nchunk, nlev, nt):
    t = pl.program_id(2)

    @pl.when(t == 0)
    def _():
        st_ref[...] = s0_ref[0, 0]

    def chunk(ci, carry):
        rows = pl.ds(pl.multiple_of(ci * c, c), c)
        zg = _dot(lr_ref[rows, :], wlr_ref[...]) + bg_ref[...]
        log_a = (jnp.minimum(zg, 0.0) - jnp.log1p(jnp.exp(-jnp.abs(zg)))) * (1.0 / GLA_TAU)
        hi, lo = _split_bf16(log_a)
        d = _dot(sum_ref[...], jnp.concatenate([hi, lo], axis=1))
        w = jnp.exp(d[:, :GLA_DKP] + d[:, GLA_DKP:])
        qb = q_ref[rows, :]
        kb = k_ref[rows, :]
        vb = v_ref[rows, :]
        q = qb.astype(F32)
        k = kb.astype(F32)
        att = pair_ref[nlev] * _dot_nt(qb, kb)
        for lev in range(nlev):
            wl = w[lev * c:(lev + 1) * c]
            att = att + pair_ref[lev] * _dot_nt((q * wl).astype(BF16), (k * wl).astype(BF16))
        w_pre = w[nlev * c:(nlev + 1) * c]
        w_suf = w[(nlev + 1) * c:(nlev + 2) * c]
        st = st_ref[...]
        o = _dot(att.astype(BF16), vb) + _dot_nt((q * w_pre).astype(BF16), st.astype(BF16))
        st_ref[...] = st * w_pre[c - 1:c, :] + _dot_tn(vb, (k * w_suf).astype(BF16))
        ms = jnp.mean(o * o, axis=-1, keepdims=True)
        on = o * lax.rsqrt(ms + RMS_EPS) * gn_ref[...]
        g = g_ref[rows, :].astype(F32)
        o_ref[rows, :] = (on * (g * _sigmoid(g))).astype(o_ref.dtype)
        return carry

    lax.fori_loop(0, nchunk, chunk, 0)

    @pl.when(t == nt - 1)
    def _():
        sf_ref[0, 0] = st_ref[...]


def _gla(proj, w_lr2p, b_gatep, g_norm, s0t, *, batch, seq):
    c = CHUNK if seq % CHUNK == 0 else seq
    tt = _tile(seq, 512)
    if tt % c:
        tt = c
    nt = seq // tt
    sums_np, pairs_np, nlev = _gla_tables(c)
    sums = jnp.asarray(sums_np, BF16)
    pairs = jnp.asarray(pairs_np, F32)
    nrow = sums_np.shape[0]

    def rowblk(b, h, t):
        return b * nt + t

    in_specs = [
        pl.BlockSpec((tt, GLA_DV), lambda b, h, t: (rowblk(b, h, t), A_OFF_V // GLA_DV + h)),
        pl.BlockSpec((tt, GLA_DV), lambda b, h, t: (rowblk(b, h, t), A_OFF_G // GLA_DV + h)),
        pl.BlockSpec((tt, GLA_DKP), lambda b, h, t: (rowblk(b, h, t), A_OFF_Q // GLA_DKP + h)),
        pl.BlockSpec((tt, GLA_DKP), lambda b, h, t: (rowblk(b, h, t), A_OFF_K // GLA_DKP + h)),
        pl.BlockSpec((tt, GLA_LRP), lambda b, h, t: (rowblk(b, h, t), A_OFF_LR // GLA_LRP)),
        pl.BlockSpec((GLA_LRP, GLA_DKP), lambda b, h, t: (0, h)),
        pl.BlockSpec((1, GLA_DKP), lambda b, h, t: (0, h)),
        pl.BlockSpec((1, GLA_DV), lambda b, h, t: (0, 0)),
        pl.BlockSpec((nrow, c), lambda b, h, t: (0, 0)),
        pl.BlockSpec((nlev + 1, c, c), lambda b, h, t: (0, 0, 0)),
        pl.BlockSpec((1, 1, GLA_DV, GLA_DKP), lambda b, h, t: (b, h, 0, 0)),
    ]
    blocks = (3 * _nbytes((tt, GLA_DV), BF16) + 2 * _nbytes((tt, GLA_DKP), BF16) + _nbytes((tt, GLA_LRP), BF16)
              + 2 * _nbytes((GLA_DV, GLA_DKP), F32) + _nbytes((nrow, c), BF16) + _nbytes((nlev + 1, c, c), F32))
    return pl.pallas_call(
        functools.partial(_gla_kernel, c=c, nchunk=tt // c, nlev=nlev, nt=nt),
        grid=(batch, GLA_HEADS, nt),
        in_specs=in_specs,
        out_specs=[pl.BlockSpec((tt, GLA_DV), lambda b, h, t: (rowblk(b, h, t), h)),
                   pl.BlockSpec((1, 1, GLA_DV, GLA_DKP), lambda b, h, t: (b, h, 0, 0))],
        out_shape=[jax.ShapeDtypeStruct((batch * seq, MAIN_WIDTH), BF16),
                   jax.ShapeDtypeStruct((batch, GLA_HEADS, GLA_DV, GLA_DKP), F32)],
        scratch_shapes=[pltpu.VMEM((GLA_DV, GLA_DKP), F32)],
        compiler_params=pltpu.CompilerParams(
            dimension_semantics=("arbitrary", "arbitrary", "arbitrary"),
            vmem_limit_bytes=_vmem_limit(blocks, scratch_bytes=_nbytes((GLA_DV, GLA_DKP), F32),
                                         temp_bytes=8 << 20)),
        name="gla_scan",
    )(proj, proj, proj, proj, proj, w_lr2p, b_gatep, g_norm, sums, pairs, s0t)


def _sb_kernel(q_ref, k_ref, v_ref, u_ref, o_ref, acc_ref, carry_ref, *, tq, tk, q_off):
    qi = pl.program_id(2)
    q = q_ref[...]
    jd = (q_off + qi * tq) // tk
    q_pos = q_off + qi * tq + lax.broadcasted_iota(jnp.int32, (tq, tk), 0)
    ntile = tk // V7X_LANES

    def step(j, masked):
        keys = pl.ds(pl.multiple_of(j * tk, tk), tk)
        z = _dot_nt(q, k_ref[keys, :])
        sp = _softplus(z)
        if masked:
            reach = (j * tk + lax.broadcasted_iota(jnp.int32, (tq, tk), 1)) < q_pos
            sp = jnp.where(reach, sp, 0.0)
        hi, lo = _split_bf16(sp)
        cs = _dot(hi, u_ref[...]) + _dot(lo, u_ref[...])
        carry = carry_ref[...]
        w = jnp.exp(z - cs - jnp.concatenate([carry] * ntile, axis=1))
        if masked:
            w = jnp.where(reach, w, 0.0)
        acc_ref[...] += _dot(w.astype(BF16), v_ref[keys, :])
        carry_ref[...] = carry + jnp.sum(sp, axis=1, keepdims=True)

    acc_ref[...] = jnp.zeros_like(acc_ref)
    carry_ref[...] = jnp.zeros_like(carry_ref)
    step(jd, True)

    def body(it, c):
        step(jd - 1 - it, False)
        return c

    lax.fori_loop(0, jd, body, 0)
    o_ref[...] = acc_ref[...].astype(o_ref.dtype)


def _sb_attention(q_arr, q_col0, k_arr, v_arr, *, batch, seq_q, seq_k, q_off):
    tq = min(SB_TILE, seq_q)
    tk = SB_TILE
    assert seq_q % tq == 0 and seq_k % tk == 0 and q_off % tk == 0 and tq <= tk
    assert q_col0 % HEAD_DIM == 0
    nq = seq_q // tq
    u = jnp.asarray(np.tril(np.ones((tk, tk), np.float32)), BF16)
    blocks = 2 * _nbytes((tq, HEAD_DIM), BF16) + 2 * _nbytes((seq_k, HEAD_DIM), BF16) + _nbytes((tk, tk), BF16)
    return pl.pallas_call(
        functools.partial(_sb_kernel, tq=tq, tk=tk, q_off=q_off),
        grid=(batch, SB_HEADS, nq),
        in_specs=[pl.BlockSpec((tq, HEAD_DIM), lambda b, h, i: (b * nq + i, q_col0 // HEAD_DIM + h)),
                  pl.BlockSpec((seq_k, HEAD_DIM), lambda b, h, i: (b, h)),
                  pl.BlockSpec((seq_k, HEAD_DIM), lambda b, h, i: (b, h)),
                  pl.BlockSpec((tk, tk), lambda b, h, i: (0, 0))],
        out_specs=pl.BlockSpec((tq, HEAD_DIM), lambda b, h, i: (b * nq + i, h)),
        out_shape=jax.ShapeDtypeStruct((batch * seq_q, MAIN_WIDTH), BF16),
        scratch_shapes=[pltpu.VMEM((tq, HEAD_DIM), F32), pltpu.VMEM((tq, V7X_LANES), F32)],
        compiler_params=pltpu.CompilerParams(
            dimension_semantics=("arbitrary", "arbitrary", "arbitrary"),
            vmem_limit_bytes=_vmem_limit(blocks, temp_bytes=12 * _nbytes((tq, tk), F32))),
        name="stick_breaking",
    )(q_arr, k_arr, v_arr, u)


def _mem_kernel(qm_ref, mk_ref, mv_ref, o_ref):
    for h in range(MEM_HEADS):
        cols = slice(h * HEAD_DIM, (h + 1) * HEAD_DIM)
        s = _dot_nt(qm_ref[:, cols], mk_ref[0, :, cols])
        e = jnp.exp(s - jnp.max(s, axis=-1, keepdims=True))
        den = jnp.sum(e, axis=-1, keepdims=True)
        o = _dot(e.astype(BF16), mv_ref[0, :, cols]) / den
        o_ref[:, cols] = o.astype(o_ref.dtype)


def _mem_attention(q_arr, q_col0, mk, mv, *, batch, seq):
    tt = _tile(seq, 512)
    nt = seq // tt
    mlen = mk.shape[1]
    assert q_col0 % MEM_WIDTH == 0
    blocks = 2 * _nbytes((tt, MEM_WIDTH), BF16) + 2 * _nbytes((mlen, MEM_WIDTH), BF16)
    return pl.pallas_call(
        _mem_kernel,
        grid=(batch, nt),
        in_specs=[pl.BlockSpec((tt, MEM_WIDTH), lambda b, t: (b * nt + t, q_col0 // MEM_WIDTH)),
                  pl.BlockSpec((1, mlen, MEM_WIDTH), lambda b, t: (b, 0, 0)),
                  pl.BlockSpec((1, mlen, MEM_WIDTH), lambda b, t: (b, 0, 0))],
        out_specs=pl.BlockSpec((tt, MEM_WIDTH), lambda b, t: (b * nt + t, 0)),
        out_shape=jax.ShapeDtypeStruct((batch * seq, MEM_WIDTH), BF16),
        compiler_params=pltpu.CompilerParams(
            dimension_semantics=("arbitrary", "arbitrary"),
            vmem_limit_bytes=_vmem_limit(blocks, temp_bytes=6 * _nbytes((tt, mlen), F32))),
        name="mem_attention",
    )(q_arr, mk, mv)


def _pad_heads(w, width, padded):
    lead = w.shape[:-1]
    w = w.reshape(lead + (GLA_HEADS, width))
    w = jnp.pad(w, [(0, 0)] * len(lead) + [(0, 0), (0, padded - width)])
    return w.reshape(lead + (GLA_HEADS * padded,))


def _layout_in_a(w_in):
    q = w_in[:, :GLA_KW]
    k = w_in[:, GLA_KW:2 * GLA_KW]
    v = w_in[:, 2 * GLA_KW:2 * GLA_KW + MAIN_WIDTH]
    g = w_in[:, 2 * GLA_KW + MAIN_WIDTH:2 * GLA_KW + 2 * MAIN_WIDTH]
    lr = w_in[:, 2 * GLA_KW + 2 * MAIN_WIDTH:2 * GLA_KW + 2 * MAIN_WIDTH + GLA_LOWRANK]
    qm = w_in[:, 2 * GLA_KW + 2 * MAIN_WIDTH + GLA_LOWRANK:]
    lr = jnp.pad(lr, ((0, 0), (0, GLA_LRP - GLA_LOWRANK)))
    w = jnp.concatenate([v, g, _pad_heads(q, GLA_DK, GLA_DKP), _pad_heads(k, GLA_DK, GLA_DKP), qm, lr], axis=1)
    scale = np.ones((1, A_WIDTH), np.float32)
    scale[:, A_OFF_Q:A_OFF_K] = GLA_DK ** -0.5
    scale[:, A_OFF_QM:A_OFF_LR] = MEM_SCALE
    return w.astype(BF16), jnp.asarray(scale)


def _scale_in_b():
    scale = np.ones((1, D_MODEL), np.float32)
    scale[:, :MAIN_WIDTH] = SB_SCALE
    scale[:, MAIN_WIDTH:] = MEM_SCALE
    return jnp.asarray(scale)


def _finish_layer(x, o_main, o_mem, lw):
    x1, x1b = _matmul_residual_ln([(o_main, lw["w_o_main"]), (o_mem, lw["w_o_mem"])], x,
                                  lw["ln1_g"], lw["ln1_b"], tm=512)
    hid = _ffn_in(x1b, lw["w_ffn_in"])
    return _matmul_residual_ln([(hid, lw["w_ffn_out"])], x1, lw["ln2_g"], lw["ln2_b"], tm=512, tk=D_FF // 4)


def _layer_a(x, xb, lw, mk, mv, s0t, *, batch, seq):
    proj, = _matmul(xb, lw["w_in"], out_dtypes=(BF16,), col_scale=lw["in_scale"], tm=1024, tn=1920)
    o_main, sft = _gla(proj, lw["w_lr2p"], lw["b_gatep"], lw["g_norm"], s0t, batch=batch, seq=seq)
    o_mem = _mem_attention(proj, A_OFF_QM, mk, mv, batch=batch, seq=seq)
    x, xb = _finish_layer(x, o_main, o_mem, lw)
    return x, xb, sft


def _layer_b(x, xb, lw, mk, mv, kb, vb, *, batch, seq, seq_k, q_off):
    proj, = _matmul(xb, lw["w_in"], out_dtypes=(BF16,), col_scale=lw["in_scale"], tm=1024, tn=1024)
    o_main = _sb_attention(proj, 0, kb, vb, batch=batch, seq_q=seq, seq_k=seq_k, q_off=q_off)
    o_mem = _mem_attention(proj, MAIN_WIDTH, mk, mv, batch=batch, seq=seq)
    return _finish_layer(x, o_main, o_mem, lw)


def _state_to_kernel(s):
    st = jnp.swapaxes(s, -1, -2)
    return jnp.pad(st, ((0, 0), (0, 0), (0, 0), (0, GLA_DKP - GLA_DK)))


def _state_from_kernel(st):
    return jnp.swapaxes(st[..., :GLA_DK], -1, -2)


def kernel(x_prompt, x_sample, mem_prompt, state_gla, cache_sb_k, cache_sb_v, cache_mem_k, cache_mem_v,
           w_in_a, w_gate_lr, b_gate, gla_norm_g, w_in_b, w_kv_shared, w_mem_kv, w_o, ln1_g, ln1_b,
           ln2_g, ln2_b, w_ffn_in, w_ffn_out):
    bp, lp, _ = x_prompt.shape
    bs, ls, _ = x_sample.shape
    mlen = mem_prompt.shape[1]

    layers = []
    for l in range(DEPTH):
        lw = {
            "w_o_main": w_o[l, :MAIN_WIDTH].astype(BF16),
            "w_o_mem": w_o[l, MAIN_WIDTH:].astype(BF16),
            "ln1_g": ln1_g[l][None, :], "ln1_b": ln1_b[l][None, :],
            "ln2_g": ln2_g[l][None, :], "ln2_b": ln2_b[l][None, :],
            "w_ffn_in": w_ffn_in[l].astype(BF16),
            "w_ffn_out": w_ffn_out[l].astype(BF16),
        }
        if l < N_A:
            lw["w_in"], lw["in_scale"] = _layout_in_a(w_in_a[l])
            lr2 = _pad_heads(w_gate_lr[l], GLA_DK, GLA_DKP)
            lw["w_lr2p"] = jnp.pad(lr2, ((0, GLA_LRP - GLA_LOWRANK), (0, 0))).astype(BF16)
            lw["b_gatep"] = _pad_heads(b_gate[l][None, :], GLA_DK, GLA_DKP)
            lw["g_norm"] = gla_norm_g[l][None, :]
        else:
            lw["w_in"], lw["in_scale"] = w_in_b[l - N_A].astype(BF16), _scale_in_b()
        layers.append(lw)
    w_kv_b = w_kv_shared.astype(BF16)
    w_memkv_b = w_mem_kv.astype(BF16)

    xp = x_prompt.reshape(bp * lp, D_MODEL)
    xs = x_sample.reshape(bs * ls, D_MODEL)
    xpb, xsb = xp.astype(BF16), xs.astype(BF16)
    memb = mem_prompt.reshape(bp * mlen, D_MODEL).astype(BF16)
    cmk = cache_mem_k.reshape(DEPTH, bs, mlen, MEM_WIDTH).astype(BF16)
    cmv = cache_mem_v.reshape(DEPTH, bs, mlen, MEM_WIDTH).astype(BF16)

    s_zero = jnp.zeros((bp, GLA_HEADS, GLA_DV, GLA_DKP), F32)
    gla_p, gla_s, mk_list, mv_list = [], [], [], []
    for l in range(DEPTH):
        lw = layers[l]
        mkv, = _matmul(memb, w_memkv_b[l], out_dtypes=(F32,), tm=1024, tn=1024)
        mk_p, mv_p = mkv[:, :MEM_WIDTH], mkv[:, MEM_WIDTH:]
        mk_list.append(mk_p.reshape(bp, mlen, MEM_HEADS, HEAD_DIM))
        mv_list.append(mv_p.reshape(bp, mlen, MEM_HEADS, HEAD_DIM))
        mk_pb = mk_p.astype(BF16).reshape(bp, mlen, MEM_WIDTH)
        mv_pb = mv_p.astype(BF16).reshape(bp, mlen, MEM_WIDTH)
        if l < N_A:
            xp, xpb, sp = _layer_a(xp, xpb, lw, mk_pb, mv_pb, s_zero, batch=bp, seq=lp)
            xs, xsb, ss = _layer_a(xs, xsb, lw, cmk[l], cmv[l], _state_to_kernel(state_gla[l]),
                                   batch=bs, seq=ls)
            gla_p.append(_state_from_kernel(sp))
            gla_s.append(_state_from_kernel(ss))
        else:
            if l == N_A:
                half = MAIN_WIDTH
                kp_f, kp_b = _matmul(xpb, w_kv_b[:, :half], out_dtypes=(F32, BF16), tm=1024, tn=768)
                vp_f, vp_b = _matmul(xpb, w_kv_b[:, half:], out_dtypes=(F32, BF16), tm=1024, tn=768)
                ks_f, ks_b = _matmul(xsb, w_kv_b[:, :half], out_dtypes=(F32, BF16), tm=1024, tn=768)
                vs_f, vs_b = _matmul(xsb, w_kv_b[:, half:], out_dtypes=(F32, BF16), tm=1024, tn=768)
                lk = PAST_LEN + ls
                lk_pad = -(-lk // SB_TILE) * SB_TILE

                def with_cache(cache, new_b):
                    allk = jnp.concatenate([cache.reshape(bs, PAST_LEN, MAIN_WIDTH).astype(BF16),
                                            new_b.reshape(bs, ls, MAIN_WIDTH)], axis=1)
                    allk = jnp.pad(allk, ((0, 0), (0, lk_pad - lk), (0, 0)))
                    return allk.reshape(bs * lk_pad, MAIN_WIDTH)

                k_all_b = with_cache(cache_sb_k, ks_b)
                v_all_b = with_cache(cache_sb_v, vs_b)
            xp, xpb = _layer_b(xp, xpb, lw, mk_pb, mv_pb, kp_b, vp_b, batch=bp, seq=lp, seq_k=lp, q_off=0)
            xs, xsb = _layer_b(xs, xsb, lw, cmk[l], cmv[l], k_all_b, v_all_b, batch=bs, seq=ls,
                               seq_k=lk_pad, q_off=PAST_LEN)

    return (xp.reshape(bp, lp, D_MODEL), xs.reshape(bs, ls, D_MODEL),
            jnp.stack(gla_p, 0), jnp.stack(gla_s, 0),
            kp_f.reshape(bp, lp, SB_HEADS, HEAD_DIM), vp_f.reshape(bp, lp, SB_HEADS, HEAD_DIM),
            ks_f.reshape(bs, ls, SB_HEADS, HEAD_DIM), vs_f.reshape(bs, ls, SB_HEADS, HEAD_DIM),
            jnp.stack(mk_list, 0), jnp.stack(mv_list, 0))
```

```python
import functools
import math

import numpy as np
import jax
import jax.numpy as jnp
from jax import lax
from jax.experimental import pallas as pl
from jax.experimental.pallas import tpu as pltpu

D_MODEL = 2048
DEPTH = 4
PAST_LEN = 2048
CHUNK = 64
N_A = DEPTH // 2
HEAD_DIM = 128
MEM_HEADS = 4
MEM_WIDTH = MEM_HEADS * HEAD_DIM
MAIN_WIDTH = D_MODEL - MEM_WIDTH
GLA_HEADS = 4
GLA_DV = MAIN_WIDTH // GLA_HEADS
GLA_DK = GLA_DV // 2
GLA_KW = GLA_HEADS * GLA_DK
GLA_LOWRANK = 16
GLA_TAU = 16.0
SB_HEADS = MAIN_WIDTH // HEAD_DIM
D_FF = ((8 * D_MODEL // 3 + 255) // 256) * 256
DN_ALPHA = (2 * DEPTH) ** 0.25
LN_EPS = 1e-5
RMS_EPS = 1e-6
SB_SCALE = HEAD_DIM ** -0.5
MEM_SCALE = HEAD_DIM ** -0.5

V7X_LANES = 128
V7X_MXU_DIM = 256
V7X_VMEM_BYTES = 64 * 1024 * 1024

GLA_DKP = V7X_MXU_DIM
GLA_LRP = V7X_LANES
A_OFF_V = 0
A_OFF_G = MAIN_WIDTH
A_OFF_Q = 2 * MAIN_WIDTH
A_OFF_K = A_OFF_Q + GLA_HEADS * GLA_DKP
A_OFF_QM = A_OFF_K + GLA_HEADS * GLA_DKP
A_OFF_LR = A_OFF_QM + MEM_WIDTH
A_WIDTH = A_OFF_LR + GLA_LRP

SB_TILE = 256
SB_DEAD = 104.0

F32 = jnp.float32
BF16 = jnp.bfloat16


def _vmem_limit(block_bytes, scratch_bytes=0, temp_bytes=0):
    need = 2 * block_bytes + scratch_bytes + temp_bytes + (2 << 20)
    return int(min(max(need, 16 << 20), V7X_VMEM_BYTES - (8 << 20)))


def _nbytes(shape, dtype):
    return int(np.prod(shape)) * jnp.dtype(dtype).itemsize


def _tile(n, pref):
    if n <= pref:
        return n
    t = pref
    while n % t:
        t -= 1
    return t


def _dot(a, b):
    return jnp.dot(a, b, preferred_element_type=F32)


def _dot_nt(a, b):
    return lax.dot_general(a, b, (((1,), (1,)), ((), ())), preferred_element_type=F32)


def _dot_tn(a, b):
    return lax.dot_general(a, b, (((0,), (0,)), ((), ())), preferred_element_type=F32)


def _sigmoid(x):
    return 1.0 / (1.0 + jnp.exp(-x))


def _softplus(x):
    return jnp.maximum(x, 0.0) + jnp.log(1.0 + jnp.exp(-jnp.abs(x)))


def _split_bf16(x):
    hi = x.astype(BF16)
    lo = (x - hi.astype(F32)).astype(BF16)
    return hi, lo


def _mm_kernel(*refs, has_scale):
    if has_scale:
        a_ref, w_ref, s_ref = refs[:3]
        o_refs = refs[3:]
    else:
        a_ref, w_ref = refs[:2]
        o_refs = refs[2:]
    acc = _dot(a_ref[...], w_ref[...])
    if has_scale:
        acc = acc * s_ref[...]
    for o_ref in o_refs:
        o_ref[...] = acc.astype(o_ref.dtype)


def _matmul(a, w, *, out_dtypes, col_scale=None, tm=1024, tn=1024):
    m, k = a.shape
    n = w.shape[1]
    tm, tn = _tile(m, tm), _tile(n, tn)
    in_specs = [pl.BlockSpec((tm, k), lambda j, i: (i, 0)),
                pl.BlockSpec((k, tn), lambda j, i: (0, j))]
    args = [a, w]
    if col_scale is not None:
        in_specs.append(pl.BlockSpec((1, tn), lambda j, i: (0, j)))
        args.append(col_scale)
    blocks = _nbytes((tm, k), BF16) + _nbytes((k, tn), BF16) + sum(_nbytes((tm, tn), d) for d in out_dtypes)
    outs = pl.pallas_call(
        functools.partial(_mm_kernel, has_scale=col_scale is not None),
        grid=(n // tn, m // tm),
        in_specs=in_specs,
        out_specs=[pl.BlockSpec((tm, tn), lambda j, i: (i, j)) for _ in out_dtypes],
        out_shape=[jax.ShapeDtypeStruct((m, n), d) for d in out_dtypes],
        compiler_params=pltpu.CompilerParams(
            dimension_semantics=("arbitrary", "arbitrary"),
            vmem_limit_bytes=_vmem_limit(blocks, temp_bytes=2 * _nbytes((tm, tn), F32))),
        name="proj_matmul",
    )(*args)
    return outs


def _swiglu_kernel(a_ref, wg_ref, wu_ref, o_ref):
    a = a_ref[...]
    gate = _dot(a, wg_ref[...])
    up = _dot(a, wu_ref[...])
    o_ref[...] = (gate * _sigmoid(gate) * up).astype(o_ref.dtype)


def _ffn_in(a, w, *, tm=2048, tn=512):
    m, k = a.shape
    tm, tn = _tile(m, tm), _tile(D_FF, tn)
    nj = D_FF // tn
    blocks = _nbytes((tm, k), BF16) + 2 * _nbytes((k, tn), BF16) + _nbytes((tm, tn), BF16)
    return pl.pallas_call(
        _swiglu_kernel,
        grid=(nj, m // tm),
        in_specs=[pl.BlockSpec((tm, k), lambda j, i: (i, 0)),
                  pl.BlockSpec((k, tn), lambda j, i: (0, j)),
                  pl.BlockSpec((k, tn), lambda j, i: (0, j + nj))],
        out_specs=pl.BlockSpec((tm, tn), lambda j, i: (i, j)),
        out_shape=jax.ShapeDtypeStruct((m, D_FF), BF16),
        compiler_params=pltpu.CompilerParams(
            dimension_semantics=("arbitrary", "arbitrary"),
            vmem_limit_bytes=_vmem_limit(blocks, temp_bytes=4 * _nbytes((tm, tn), F32))),
        name="ffn_in_swiglu",
    )(a, w, w)


def _ln_kernel(*refs, npair, nk):
    aw_refs = refs[:2 * npair]
    x_ref, g_ref, b_ref, of_ref, ob_ref = refs[2 * npair:2 * npair + 5]
    scratch = refs[2 * npair + 5:]

    def finish(acc):
        y = DN_ALPHA * x_ref[...] + acc
        mu = jnp.mean(y, axis=-1, keepdims=True)
        d = y - mu
        var = jnp.mean(d * d, axis=-1, keepdims=True)
        out = d * lax.rsqrt(var + LN_EPS) * g_ref[...] + b_ref[...]
        of_ref[...] = out
        ob_ref[...] = out.astype(ob_ref.dtype)

    part = _dot(aw_refs[0][...], aw_refs[1][...])
    for p in range(1, npair):
        part = part + _dot(aw_refs[2 * p][...], aw_refs[2 * p + 1][...])
    if nk == 1:
        finish(part)
        return
    acc_ref, = scratch
    kk = pl.program_id(1)

    @pl.when(kk == 0)
    def _():
        acc_ref[...] = part

    @pl.when(kk > 0)
    def _():
        acc_ref[...] += part

    @pl.when(kk == nk - 1)
    def _():
        finish(acc_ref[...])


def _matmul_residual_ln(pairs, x, gamma, beta, *, tm=512, tk=None):
    m, n = x.shape
    tm = _tile(m, tm)
    npair = len(pairs)
    nk = 1
    if npair == 1 and tk is not None:
        k = pairs[0][0].shape[1]
        tk = _tile(k, tk)
        nk = k // tk
    in_specs, args, blocks = [], [], 0
    for a, w in pairs:
        ka = tk if nk > 1 else a.shape[1]
        in_specs += [pl.BlockSpec((tm, ka), lambda i, kk: (i, kk)),
                     pl.BlockSpec((ka, n), lambda i, kk: (kk, 0))]
        args += [a, w]
        blocks += _nbytes((tm, ka), BF16) + _nbytes((ka, n), BF16)
    in_specs += [pl.BlockSpec((tm, n), lambda i, kk: (i, 0)),
                 pl.BlockSpec((1, n), lambda i, kk: (0, 0)),
                 pl.BlockSpec((1, n), lambda i, kk: (0, 0))]
    blocks += 2 * _nbytes((tm, n), F32) + _nbytes((tm, n), BF16)
    acc_bytes = _nbytes((tm, n), F32) if nk > 1 else 0
    return pl.pallas_call(
        functools.partial(_ln_kernel, npair=npair, nk=nk),
        grid=(m // tm, nk),
        in_specs=in_specs,
        out_specs=[pl.BlockSpec((tm, n), lambda i, kk: (i, 0)),
                   pl.BlockSpec((tm, n), lambda i, kk: (i, 0))],
        out_shape=[jax.ShapeDtypeStruct((m, n), F32), jax.ShapeDtypeStruct((m, n), BF16)],
        scratch_shapes=[pltpu.VMEM((tm, n), F32)] if nk > 1 else [],
        compiler_params=pltpu.CompilerParams(
            dimension_semantics=("arbitrary", "arbitrary"),
            vmem_limit_bytes=_vmem_limit(blocks, scratch_bytes=acc_bytes,
                                         temp_bytes=3 * _nbytes((tm, n), F32))),
        name="matmul_residual_ln",
    )(*args, x, gamma, beta)


def _gla_tables(c):
    nlev = int(math.log2(c))
    assert 1 << nlev == c
    idx = np.arange(c)
    i, t = idx[:, None], idx[None, :]
    sums, pairs = [], []
    for lev in range(nlev):
        h = c >> (lev + 1)
        p = (i // (2 * h)) * 2 * h + h - 1
        second = (i % (2 * h)) >= h
        sums.append(np.where(second, (t > p) & (t <= i), (t > i) & (t <= p)))
        pairs.append((i // (2 * h) == t // (2 * h)) & second & ((t % (2 * h)) < h))
    sums.append(t <= i)
    sums.append(t > i)
    pairs.append(i == t)
    return (np.concatenate(sums, 0).astype(np.float32), np.stack(pairs, 0).astype(np.float32), nlev)


def _gla_kernel(v_ref, g_ref, q_ref, k_ref, lr_ref, wlr_ref, bg_ref, gn_ref, sum_ref, pair_ref, s0_ref,
                o_ref, sf_ref, st_ref, *, c, nchunk, nlev, nt):
    t = pl.program_id(2)

    @pl.when(t == 0)
    def _():
        st_ref[...] = s0_ref[0, 0]

    def chunk(ci, carry):
        rows = pl.ds(pl.multiple_of(ci * c, c), c)
        zg = _dot(lr_ref[rows, :], wlr_ref[...]) + bg_ref[...]
        log_a = -_softplus(-zg) * (1.0 / GLA_TAU)
        hi, lo = _split_bf16(log_a)
        d = _dot(sum_ref[...], jnp.concatenate([hi, lo], axis=1))
        w = jnp.exp(d[:, :GLA_DKP] + d[:, GLA_DKP:])
        qb = q_ref[rows, :]
        kb = k_ref[rows, :]
        vb = v_ref[rows, :]
        q = qb.astype(F32)
        k = kb.astype(F32)
        att = pair_ref[nlev] * _dot_nt(qb, kb)
        for lev in range(nlev):
            wl = w[lev * c:(lev + 1) * c]
            att = att + pair_ref[lev] * _dot_nt((q * wl).astype(BF16), (k * wl).astype(BF16))
        w_pre = w[nlev * c:(nlev + 1) * c]
        w_suf = w[(nlev + 1) * c:(nlev + 2) * c]
        st = st_ref[...]
        o = _dot(att.astype(BF16), vb) + _dot_nt((q * w_pre).astype(BF16), st.astype(BF16))
        st_ref[...] = st * w_pre[c - 1:c, :] + _dot_tn(vb, (k * w_suf).astype(BF16))
        ms = jnp.mean(o * o, axis=-1, keepdims=True)
        on = o * lax.rsqrt(ms + RMS_EPS) * gn_ref[...]
        g = g_ref[rows, :].astype(F32)
        o_ref[rows, :] = (on * (g * _sigmoid(g))).astype(o_ref.dtype)
        return carry

    lax.fori_loop(0, nchunk, chunk, 0)

    @pl.when(t == nt - 1)
    def _():
        sf_ref[0, 0] = st_ref[...]


def _gla(proj, w_lr2p, b_gatep, g_norm, s0t, *, batch, seq):
    c = CHUNK if seq % CHUNK == 0 else seq
    tt = _tile(seq, 512)
    if tt % c:
        tt = c
    nt = seq // tt
    sums_np, pairs_np, nlev = _gla_tables(c)
    sums = jnp.asarray(sums_np, BF16)
    pairs = jnp.asarray(pairs_np, F32)
    nrow = sums_np.shape[0]

    def rowblk(b, h, t):
        return b * nt + t

    in_specs = [
        pl.BlockSpec((tt, GLA_DV), lambda b, h, t: (rowblk(b, h, t), A_OFF_V // GLA_DV + h)),
        pl.BlockSpec((tt, GLA_DV), lambda b, h, t: (rowblk(b, h, t), A_OFF_G // GLA_DV + h)),
        pl.BlockSpec((tt, GLA_DKP), lambda b, h, t: (rowblk(b, h, t), A_OFF_Q // GLA_DKP + h)),
        pl.BlockSpec((tt, GLA_DKP), lambda b, h, t: (rowblk(b, h, t), A_OFF_K // GLA_DKP + h)),
        pl.BlockSpec((tt, GLA_LRP), lambda b, h, t: (rowblk(b, h, t), A_OFF_LR // GLA_LRP)),
        pl.BlockSpec((GLA_LRP, GLA_DKP), lambda b, h, t: (0, h)),
        pl.BlockSpec((1, GLA_DKP), lambda b, h, t: (0, h)),
        pl.BlockSpec((1, GLA_DV), lambda b, h, t: (0, 0)),
        pl.BlockSpec((nrow, c), lambda b, h, t: (0, 0)),
        pl.BlockSpec((nlev + 1, c, c), lambda b, h, t: (0, 0, 0)),
        pl.BlockSpec((1, 1, GLA_DV, GLA_DKP), lambda b, h, t: (b, h, 0, 0)),
    ]
    blocks = (3 * _nbytes((tt, GLA_DV), BF16) + 2 * _nbytes((tt, GLA_DKP), BF16) + _nbytes((tt, GLA_LRP), BF16)
              + 2 * _nbytes((GLA_DV, GLA_DKP), F32) + _nbytes((nrow, c), BF16) + _nbytes((nlev + 1, c, c), F32))
    return pl.pallas_call(
        functools.partial(_gla_kernel, c=c, nchunk=tt // c, nlev=nlev, nt=nt),
        grid=(batch, GLA_HEADS, nt),
        in_specs=in_specs,
        out_specs=[pl.BlockSpec((tt, GLA_DV), lambda b, h, t: (rowblk(b, h, t), h)),
                   pl.BlockSpec((1, 1, GLA_DV, GLA_DKP), lambda b, h, t: (b, h, 0, 0))],
        out_shape=[jax.ShapeDtypeStruct((batch * seq, MAIN_WIDTH), BF16),
                   jax.ShapeDtypeStruct((batch, GLA_HEADS, GLA_DV, GLA_DKP), F32)],
        scratch_shapes=[pltpu.VMEM((GLA_DV, GLA_DKP), F32)],
        compiler_params=pltpu.CompilerParams(
            dimension_semantics=("arbitrary", "arbitrary", "arbitrary"),
            vmem_limit_bytes=_vmem_limit(blocks, scratch_bytes=_nbytes((GLA_DV, GLA_DKP), F32),
                                         temp_bytes=8 << 20)),
        name="gla_scan",
    )(proj, proj, proj, proj, proj, w_lr2p, b_gatep, g_norm, sums, pairs, s0t)


def _sb_kernel(q_ref, k_ref, v_ref, u_ref, o_ref, acc_ref, carry_ref, *, tq, tk, q_off):
    qi = pl.program_id(2)
    q = q_ref[...]
    jd = (q_off + qi * tq) // tk
    q_pos = q_off + qi * tq + lax.broadcasted_iota(jnp.int32, (tq, tk), 0)
    ntile = tk // V7X_LANES

    def tile(j, carry, reach):
        keys = pl.ds(pl.multiple_of(j * tk, tk), tk)
        z = _dot_nt(q, k_ref[keys, :])
        sp = _softplus(z)
        if reach is not None:
            sp = jnp.where(reach, sp, 0.0)
        hi, lo = _split_bf16(sp)
        cs = _dot(hi, u_ref[...]) + _dot(lo, u_ref[...])
        w = jnp.exp(z - cs - jnp.concatenate([carry] * ntile, axis=1))
        if reach is not None:
            w = jnp.where(reach, w, 0.0)
        return _dot(w.astype(BF16), v_ref[keys, :]), carry + jnp.sum(sp, axis=1, keepdims=True)

    def live(carry):
        return jnp.min(carry, axis=0, keepdims=True)[0, 0] < SB_DEAD

    k_col = lax.broadcasted_iota(jnp.int32, (tq, tk), 1)
    a0, c0 = tile(jd, jnp.zeros((tq, V7X_LANES), F32), (jd * tk + k_col) < q_pos)
    a1, c1 = tile(jnp.maximum(jd - 1, 0), c0, jnp.broadcast_to(jd, (tq, tk)) >= 1)
    acc_ref[...] = a0 + a1
    carry_ref[...] = c1

    def cond(state):
        j, alive = state
        return jnp.logical_and(j >= 0, alive)

    def body(state):
        j, _ = state
        a, c = tile(j, carry_ref[...], None)
        acc_ref[...] += a
        carry_ref[...] = c
        return j - 1, live(c)

    lax.while_loop(cond, body, (jd - 2, live(c1)))
    o_ref[...] = acc_ref[...].astype(o_ref.dtype)


def _sb_attention(q_arr, q_col0, k_arr, v_arr, *, batch, seq_q, seq_k, q_off):
    tq = min(SB_TILE, seq_q)
    tk = SB_TILE
    assert seq_q % tq == 0 and seq_k % tk == 0 and q_off % tk == 0 and tq <= tk
    assert q_col0 % HEAD_DIM == 0
    nq = seq_q // tq
    u = jnp.asarray(np.tril(np.ones((tk, tk), np.float32)), BF16)
    blocks = 2 * _nbytes((tq, HEAD_DIM), BF16) + 2 * _nbytes((seq_k, HEAD_DIM), BF16) + _nbytes((tk, tk), BF16)
    return pl.pallas_call(
        functools.partial(_sb_kernel, tq=tq, tk=tk, q_off=q_off),
        grid=(batch, SB_HEADS, nq),
        in_specs=[pl.BlockSpec((tq, HEAD_DIM), lambda b, h, i: (b * nq + i, q_col0 // HEAD_DIM + h)),
                  pl.BlockSpec((seq_k, HEAD_DIM), lambda b, h, i: (b, h)),
                  pl.BlockSpec((seq_k, HEAD_DIM), lambda b, h, i: (b, h)),
                  pl.BlockSpec((tk, tk), lambda b, h, i: (0, 0))],
        out_specs=pl.BlockSpec((tq, HEAD_DIM), lambda b, h, i: (b * nq + i, h)),
        out_shape=jax.ShapeDtypeStruct((batch * seq_q, MAIN_WIDTH), BF16),
        scratch_shapes=[pltpu.VMEM((tq, HEAD_DIM), F32), pltpu.VMEM((tq, V7X_LANES), F32)],
        compiler_params=pltpu.CompilerParams(
            dimension_semantics=("arbitrary", "arbitrary", "arbitrary"),
            vmem_limit_bytes=_vmem_limit(blocks, temp_bytes=12 * _nbytes((tq, tk), F32))),
        name="stick_breaking",
    )(q_arr, k_arr, v_arr, u)


def _mem_kernel(qm_ref, mk_ref, mv_ref, o_ref):
    for h in range(MEM_HEADS):
        cols = slice(h * HEAD_DIM, (h + 1) * HEAD_DIM)
        s = _dot_nt(qm_ref[:, cols], mk_ref[0, :, cols])
        e = jnp.exp(s - jnp.max(s, axis=-1, keepdims=True))
        den = jnp.sum(e, axis=-1, keepdims=True)
        o = _dot(e.astype(BF16), mv_ref[0, :, cols]) / den
        o_ref[:, cols] = o.astype(o_ref.dtype)


def _mem_attention(q_arr, q_col0, mk, mv, *, batch, seq):
    tt = _tile(seq, 512)
    nt = seq // tt
    mlen = mk.shape[1]
    assert q_col0 % MEM_WIDTH == 0
    blocks = 2 * _nbytes((tt, MEM_WIDTH), BF16) + 2 * _nbytes((mlen, MEM_WIDTH), BF16)
    return pl.pallas_call(
        _mem_kernel,
        grid=(batch, nt),
        in_specs=[pl.BlockSpec((tt, MEM_WIDTH), lambda b, t: (b * nt + t, q_col0 // MEM_WIDTH)),
                  pl.BlockSpec((1, mlen, MEM_WIDTH), lambda b, t: (b, 0, 0)),
                  pl.BlockSpec((1, mlen, MEM_WIDTH), lambda b, t: (b, 0, 0))],
        out_specs=pl.BlockSpec((tt, MEM_WIDTH), lambda b, t: (b * nt + t, 0)),
        out_shape=jax.ShapeDtypeStruct((batch * seq, MEM_WIDTH), BF16),
        compiler_params=pltpu.CompilerParams(
            dimension_semantics=("arbitrary", "arbitrary"),
            vmem_limit_bytes=_vmem_limit(blocks, temp_bytes=6 * _nbytes((tt, mlen), F32))),
        name="mem_attention",
    )(q_arr, mk, mv)


def _pad_heads(w, width, padded):
    lead = w.shape[:-1]
    w = w.reshape(lead + (GLA_HEADS, width))
    w = jnp.pad(w, [(0, 0)] * len(lead) + [(0, 0), (0, padded - width)])
    return w.reshape(lead + (GLA_HEADS * padded,))


def _layout_in_a(w_in):
    q = w_in[:, :GLA_KW]
    k = w_in[:, GLA_KW:2 * GLA_KW]
    v = w_in[:, 2 * GLA_KW:2 * GLA_KW + MAIN_WIDTH]
    g = w_in[:, 2 * GLA_KW + MAIN_WIDTH:2 * GLA_KW + 2 * MAIN_WIDTH]
    lr = w_in[:, 2 * GLA_KW + 2 * MAIN_WIDTH:2 * GLA_KW + 2 * MAIN_WIDTH + GLA_LOWRANK]
    qm = w_in[:, 2 * GLA_KW + 2 * MAIN_WIDTH + GLA_LOWRANK:]
    lr = jnp.pad(lr, ((0, 0), (0, GLA_LRP - GLA_LOWRANK)))
    w = jnp.concatenate([v, g, _pad_heads(q, GLA_DK, GLA_DKP), _pad_heads(k, GLA_DK, GLA_DKP), qm, lr], axis=1)
    scale = np.ones((1, A_WIDTH), np.float32)
    scale[:, A_OFF_Q:A_OFF_K] = GLA_DK ** -0.5
    scale[:, A_OFF_QM:A_OFF_LR] = MEM_SCALE
    return w.astype(BF16), jnp.asarray(scale)


def _scale_in_b():
    scale = np.ones((1, D_MODEL), np.float32)
    scale[:, :MAIN_WIDTH] = SB_SCALE
    scale[:, MAIN_WIDTH:] = MEM_SCALE
    return jnp.asarray(scale)


def _finish_layer(x, o_main, o_mem, lw):
    x1, x1b = _matmul_residual_ln([(o_main, lw["w_o_main"]), (o_mem, lw["w_o_mem"])], x,
                                  lw["ln1_g"], lw["ln1_b"], tm=512)
    hid = _ffn_in(x1b, lw["w_ffn_in"])
    return _matmul_residual_ln([(hid, lw["w_ffn_out"])], x1, lw["ln2_g"], lw["ln2_b"], tm=512, tk=D_FF // 4)


def _layer_a(x, xb, lw, mk, mv, s0t, *, batch, seq):
    proj, = _matmul(xb, lw["w_in"], out_dtypes=(BF16,), col_scale=lw["in_scale"], tm=1024, tn=1920)
    o_main, sft = _gla(proj, lw["w_lr2p"], lw["b_gatep"], lw["g_norm"], s0t, batch=batch, seq=seq)
    o_mem = _mem_attention(proj, A_OFF_QM, mk, mv, batch=batch, seq=seq)
    x, xb = _finish_layer(x, o_main, o_mem, lw)
    return x, xb, sft


def _layer_b(x, xb, lw, mk, mv, kb, vb, *, batch, seq, seq_k, q_off):
    proj, = _matmul(xb, lw["w_in"], out_dtypes=(BF16,), col_scale=lw["in_scale"], tm=1024, tn=1024)
    o_main = _sb_attention(proj, 0, kb, vb, batch=batch, seq_q=seq, seq_k=seq_k, q_off=q_off)
    o_mem = _mem_attention(proj, MAIN_WIDTH, mk, mv, batch=batch, seq=seq)
    return _finish_layer(x, o_main, o_mem, lw)


def _state_to_kernel(s):
    st = jnp.swapaxes(s, -1, -2)
    return jnp.pad(st, ((0, 0), (0, 0), (0, 0), (0, GLA_DKP - GLA_DK)))


def _state_from_kernel(st):
    return jnp.swapaxes(st[..., :GLA_DK], -1, -2)


def kernel(x_prompt, x_sample, mem_prompt, state_gla, cache_sb_k, cache_sb_v, cache_mem_k, cache_mem_v,
           w_in_a, w_gate_lr, b_gate, gla_norm_g, w_in_b, w_kv_shared, w_mem_kv, w_o, ln1_g, ln1_b,
           ln2_g, ln2_b, w_ffn_in, w_ffn_out):
    bp, lp, _ = x_prompt.shape
    bs, ls, _ = x_sample.shape
    mlen = mem_prompt.shape[1]

    layers = []
    for l in range(DEPTH):
        lw = {
            "w_o_main": w_o[l, :MAIN_WIDTH].astype(BF16),
            "w_o_mem": w_o[l, MAIN_WIDTH:].astype(BF16),
            "ln1_g": ln1_g[l][None, :], "ln1_b": ln1_b[l][None, :],
            "ln2_g": ln2_g[l][None, :], "ln2_b": ln2_b[l][None, :],
            "w_ffn_in": w_ffn_in[l].astype(BF16),
            "w_ffn_out": w_ffn_out[l].astype(BF16),
        }
        if l < N_A:
            lw["w_in"], lw["in_scale"] = _layout_in_a(w_in_a[l])
            lr2 = _pad_heads(w_gate_lr[l], GLA_DK, GLA_DKP)
            lw["w_lr2p"] = jnp.pad(lr2, ((0, GLA_LRP - GLA_LOWRANK), (0, 0))).astype(BF16)
            lw["b_gatep"] = _pad_heads(b_gate[l][None, :], GLA_DK, GLA_DKP)
            lw["g_norm"] = gla_norm_g[l][None, :]
        else:
            lw["w_in"], lw["in_scale"] = w_in_b[l - N_A].astype(BF16), _scale_in_b()
        layers.append(lw)
    w_kv_b = w_kv_shared.astype(BF16)
    w_memkv_b = w_mem_kv.astype(BF16)

    xp = x_prompt.reshape(bp * lp, D_MODEL)
    xs = x_sample.reshape(bs * ls, D_MODEL)
    xpb, xsb = xp.astype(BF16), xs.astype(BF16)
    memb = mem_prompt.reshape(bp * mlen, D_MODEL).astype(BF16)
    cmk = cache_mem_k.reshape(DEPTH, bs, mlen, MEM_WIDTH).astype(BF16)
    cmv = cache_mem_v.reshape(DEPTH, bs, mlen, MEM_WIDTH).astype(BF16)

    s_zero = jnp.zeros((bp, GLA_HEADS, GLA_DV, GLA_DKP), F32)
    gla_p, gla_s, mk_list, mv_list = [], [], [], []
    for l in range(DEPTH):
        lw = layers[l]
        mkv, = _matmul(memb, w_memkv_b[l], out_dtypes=(F32,), tm=1024, tn=1024)
        mk_p, mv_p = mkv[:, :MEM_WIDTH], mkv[:, MEM_WIDTH:]
        mk_list.append(mk_p.reshape(bp, mlen, MEM_HEADS, HEAD_DIM))
        mv_list.append(mv_p.reshape(bp, mlen, MEM_HEADS, HEAD_DIM))
        mk_pb = mk_p.astype(BF16).reshape(bp, mlen, MEM_WIDTH)
        mv_pb = mv_p.astype(BF16).reshape(bp, mlen, MEM_WIDTH)
        if l < N_A:
            xp, xpb, sp = _layer_a(xp, xpb, lw, mk_pb, mv_pb, s_zero, batch=bp, seq=lp)
            xs, xsb, ss = _layer_a(xs, xsb, lw, cmk[l], cmv[l], _state_to_kernel(state_gla[l]),
                                   batch=bs, seq=ls)
            gla_p.append(_state_from_kernel(sp))
            gla_s.append(_state_from_kernel(ss))
        else:
            if l == N_A:
                half = MAIN_WIDTH
                kp_f, kp_b = _matmul(xpb, w_kv_b[:, :half], out_dtypes=(F32, BF16), tm=1024, tn=768)
                vp_f, vp_b = _matmul(xpb, w_kv_b[:, half:], out_dtypes=(F32, BF16), tm=1024, tn=768)
                ks_f, ks_b = _matmul(xsb, w_kv_b[:, :half], out_dtypes=(F32, BF16), tm=1024, tn=768)
                vs_f, vs_b = _matmul(xsb, w_kv_b[:, half:], out_dtypes=(F32, BF16), tm=1024, tn=768)
                lk = PAST_LEN + ls
                lk_pad = -(-lk // SB_TILE) * SB_TILE

                def with_cache(cache, new_b):
                    allk = jnp.concatenate([cache.reshape(bs, PAST_LEN, MAIN_WIDTH).astype(BF16),
                                            new_b.reshape(bs, ls, MAIN_WIDTH)], axis=1)
                    allk = jnp.pad(allk, ((0, 0), (0, lk_pad - lk), (0, 0)))
                    return allk.reshape(bs * lk_pad, MAIN_WIDTH)

                k_all_b = with_cache(cache_sb_k, ks_b)
                v_all_b = with_cache(cache_sb_v, vs_b)
            xp, xpb = _layer_b(xp, xpb, lw, mk_pb, mv_pb, kp_b, vp_b, batch=bp, seq=lp, seq_k=lp, q_off=0)
            xs, xsb = _layer_b(xs, xsb, lw, cmk[l], cmv[l], k_all_b, v_all_b, batch=bs, seq=ls,
                               seq_k=lk_pad, q_off=PAST_LEN)

    return (xp.reshape(bp, lp, D_MODEL), xs.reshape(bs, ls, D_MODEL),
            jnp.stack(gla_p, 0), jnp.stack(gla_s, 0),
            kp_f.reshape(bp, lp, SB_HEADS, HEAD_DIM), vp_f.reshape(bp, lp, SB_HEADS, HEAD_DIM),
            ks_f.reshape(bs, ls, SB_HEADS, HEAD_DIM), vs_f.reshape(bs, ls, SB_HEADS, HEAD_DIM),
            jnp.stack(mk_list, 0), jnp.stack(mv_list, 0))
```

```python
import functools
import math

import numpy as np
import jax
import jax.numpy as jnp
from jax import lax
from jax.experimental import pallas as pl
from jax.experimental.pallas import tpu as pltpu

D_MODEL = 2048
DEPTH = 4
PAST_LEN = 2048
CHUNK = 64
N_A = DEPTH // 2
HEAD_DIM = 128
MEM_HEADS = 4
MEM_WIDTH = MEM_HEADS * HEAD_DIM
MAIN_WIDTH = D_MODEL - MEM_WIDTH
GLA_HEADS = 4
GLA_DV = MAIN_WIDTH // GLA_HEADS
GLA_DK = GLA_DV // 2
GLA_KW = GLA_HEADS * GLA_DK
GLA_LOWRANK = 16
GLA_TAU = 16.0
SB_HEADS = MAIN_WIDTH // HEAD_DIM
D_FF = ((8 * D_MODEL // 3 + 255) // 256) * 256
DN_ALPHA = (2 * DEPTH) ** 0.25
LN_EPS = 1e-5
RMS_EPS = 1e-6
SB_SCALE = HEAD_DIM ** -0.5
MEM_SCALE = HEAD_DIM ** -0.5

V7X_LANES = 128
V7X_MXU_DIM = 256
V7X_VMEM_BYTES = 64 * 1024 * 1024

GLA_DKP = V7X_MXU_DIM
GLA_LRP = V7X_LANES
A_OFF_V = 0
A_OFF_G = MAIN_WIDTH
A_OFF_Q = 2 * MAIN_WIDTH
A_OFF_K = A_OFF_Q + GLA_HEADS * GLA_DKP
A_OFF_QM = A_OFF_K + GLA_HEADS * GLA_DKP
A_OFF_LR = A_OFF_QM + MEM_WIDTH
A_WIDTH = A_OFF_LR + GLA_LRP

SB_TILE = 256
SB_HEADS_PER_STEP = 4
SB_DEAD_LOG2 = 150.0
LOG2_E = math.log2(math.e)

F32 = jnp.float32
BF16 = jnp.bfloat16


def _vmem_limit(block_bytes, scratch_bytes=0, temp_bytes=0):
    need = 2 * block_bytes + scratch_bytes + temp_bytes + (2 << 20)
    return int(min(max(need, 16 << 20), V7X_VMEM_BYTES - (8 << 20)))


def _nbytes(shape, dtype):
    return int(np.prod(shape)) * jnp.dtype(dtype).itemsize


def _tile(n, pref):
    if n <= pref:
        return n
    t = pref
    while n % t:
        t -= 1
    return t


def _dot(a, b):
    return jnp.dot(a, b, preferred_element_type=F32)


def _dot_nt(a, b):
    return lax.dot_general(a, b, (((1,), (1,)), ((), ())), preferred_element_type=F32)


def _dot_tn(a, b):
    return lax.dot_general(a, b, (((0,), (0,)), ((), ())), preferred_element_type=F32)


def _sigmoid(x):
    return 1.0 / (1.0 + jnp.exp(-x))


def _softplus(x):
    return jnp.maximum(x, 0.0) + jnp.log(1.0 + jnp.exp(-jnp.abs(x)))


def _mm_kernel(*refs, has_scale):
    if has_scale:
        a_ref, w_ref, s_ref = refs[:3]
        o_refs = refs[3:]
    else:
        a_ref, w_ref = refs[:2]
        o_refs = refs[2:]
    acc = _dot(a_ref[...].astype(BF16), w_ref[...])
    if has_scale:
        acc = acc * s_ref[...]
    for o_ref in o_refs:
        o_ref[...] = acc.astype(o_ref.dtype)


def _matmul(a, w, *, out_dtypes, col_scale=None, tm=1024, tn=1024):
    m, k = a.shape
    n = w.shape[1]
    tm, tn = _tile(m, tm), _tile(n, tn)
    in_specs = [pl.BlockSpec((tm, k), lambda j, i: (i, 0)),
                pl.BlockSpec((k, tn), lambda j, i: (0, j))]
    args = [a, w]
    if col_scale is not None:
        in_specs.append(pl.BlockSpec((1, tn), lambda j, i: (0, j)))
        args.append(col_scale)
    blocks = _nbytes((tm, k), a.dtype) + _nbytes((k, tn), BF16) + sum(_nbytes((tm, tn), d) for d in out_dtypes)
    outs = pl.pallas_call(
        functools.partial(_mm_kernel, has_scale=col_scale is not None),
        grid=(n // tn, m // tm),
        in_specs=in_specs,
        out_specs=[pl.BlockSpec((tm, tn), lambda j, i: (i, j)) for _ in out_dtypes],
        out_shape=[jax.ShapeDtypeStruct((m, n), d) for d in out_dtypes],
        compiler_params=pltpu.CompilerParams(
            dimension_semantics=("arbitrary", "arbitrary"),
            vmem_limit_bytes=_vmem_limit(blocks, temp_bytes=2 * _nbytes((tm, tn), F32))),
        name="proj_matmul",
    )(*args)
    return outs


def _swiglu_kernel(a_ref, wg_ref, wu_ref, o_ref):
    a = a_ref[...]
    gate = _dot(a, wg_ref[...])
    up = _dot(a, wu_ref[...])
    o_ref[...] = (gate * _sigmoid(gate) * up).astype(o_ref.dtype)


def _ffn_in(a, w, *, tm=2048, tn=512):
    m, k = a.shape
    tm, tn = _tile(m, tm), _tile(D_FF, tn)
    nj = D_FF // tn
    blocks = _nbytes((tm, k), BF16) + 2 * _nbytes((k, tn), BF16) + _nbytes((tm, tn), BF16)
    return pl.pallas_call(
        _swiglu_kernel,
        grid=(nj, m // tm),
        in_specs=[pl.BlockSpec((tm, k), lambda j, i: (i, 0)),
                  pl.BlockSpec((k, tn), lambda j, i: (0, j)),
                  pl.BlockSpec((k, tn), lambda j, i: (0, j + nj))],
        out_specs=pl.BlockSpec((tm, tn), lambda j, i: (i, j)),
        out_shape=jax.ShapeDtypeStruct((m, D_FF), BF16),
        compiler_params=pltpu.CompilerParams(
            dimension_semantics=("arbitrary", "arbitrary"),
            vmem_limit_bytes=_vmem_limit(blocks, temp_bytes=4 * _nbytes((tm, tn), F32))),
        name="ffn_in_swiglu",
    )(a, w, w)


def _ln_kernel(*refs, npair, nk):
    aw_refs = refs[:2 * npair]
    x_ref, g_ref, b_ref, of_ref, ob_ref = refs[2 * npair:2 * npair + 5]
    scratch = refs[2 * npair + 5:]

    def finish(acc):
        y = DN_ALPHA * x_ref[...] + acc
        mu = jnp.mean(y, axis=-1, keepdims=True)
        d = y - mu
        var = jnp.mean(d * d, axis=-1, keepdims=True)
        out = d * lax.rsqrt(var + LN_EPS) * g_ref[...] + b_ref[...]
        of_ref[...] = out
        ob_ref[...] = out.astype(ob_ref.dtype)

    part = _dot(aw_refs[0][...], aw_refs[1][...])
    for p in range(1, npair):
        part = part + _dot(aw_refs[2 * p][...], aw_refs[2 * p + 1][...])
    if nk == 1:
        finish(part)
        return
    acc_ref, = scratch
    kk = pl.program_id(1)

    @pl.when(kk == 0)
    def _():
        acc_ref[...] = part

    @pl.when(kk > 0)
    def _():
        acc_ref[...] += part

    @pl.when(kk == nk - 1)
    def _():
        finish(acc_ref[...])


def _matmul_residual_ln(pairs, x, gamma, beta, *, tm=512, tk=None):
    m, n = x.shape
    tm = _tile(m, tm)
    npair = len(pairs)
    nk = 1
    if npair == 1 and tk is not None:
        k = pairs[0][0].shape[1]
        tk = _tile(k, tk)
        nk = k // tk
    in_specs, args, blocks, resident = [], [], 0, 0
    for a, w in pairs:
        ka = tk if nk > 1 else a.shape[1]
        in_specs.append(pl.BlockSpec((tm, ka), lambda i, kk: (i, kk)))
        if nk > 1:
            in_specs.append(pl.BlockSpec((ka, n), lambda i, kk: (kk, 0)))
            blocks += _nbytes((ka, n), BF16)
        else:
            in_specs.append(pl.BlockSpec((ka, n), lambda i, kk: (0, 0), pipeline_mode=pl.Buffered(1)))
            resident += _nbytes((ka, n), BF16)
        args += [a, w]
        blocks += _nbytes((tm, ka), BF16)
    in_specs += [pl.BlockSpec((tm, n), lambda i, kk: (i, 0)),
                 pl.BlockSpec((1, n), lambda i, kk: (0, 0)),
                 pl.BlockSpec((1, n), lambda i, kk: (0, 0))]
    blocks += 2 * _nbytes((tm, n), F32) + _nbytes((tm, n), BF16)
    acc_bytes = _nbytes((tm, n), F32) if nk > 1 else 0
    return pl.pallas_call(
        functools.partial(_ln_kernel, npair=npair, nk=nk),
        grid=(m // tm, nk),
        in_specs=in_specs,
        out_specs=[pl.BlockSpec((tm, n), lambda i, kk: (i, 0)),
                   pl.BlockSpec((tm, n), lambda i, kk: (i, 0))],
        out_shape=[jax.ShapeDtypeStruct((m, n), F32), jax.ShapeDtypeStruct((m, n), BF16)],
        scratch_shapes=[pltpu.VMEM((tm, n), F32)] if nk > 1 else [],
        compiler_params=pltpu.CompilerParams(
            dimension_semantics=("arbitrary", "arbitrary"),
            vmem_limit_bytes=_vmem_limit(blocks, scratch_bytes=acc_bytes + resident,
                                         temp_bytes=3 * _nbytes((tm, n), F32))),
        name="matmul_residual_ln",
    )(*args, x, gamma, beta)


def _gla_tables(c):
    nlev = int(math.log2(c))
    assert 1 << nlev == c
    idx = np.arange(c)
    i, t = idx[:, None], idx[None, :]
    sums, pairs = [], []
    for lev in range(nlev):
        h = c >> (lev + 1)
        p = (i // (2 * h)) * 2 * h + h - 1
        second = (i % (2 * h)) >= h
        sums.append(np.where(second, (t > p) & (t <= i), (t > i) & (t <= p)))
        pairs.append((i // (2 * h) == t // (2 * h)) & second & ((t % (2 * h)) < h))
    sums.append(t <= i)
    sums.append(t > i)
    pairs.append(i == t)
    return (np.concatenate(sums, 0).astype(np.float32), np.stack(pairs, 0).astype(np.float32), nlev)


def _gla_kernel(v_ref, g_ref, q_ref, k_ref, lr_ref, wlr_ref, bg_ref, gn_ref, sum_ref, pair_ref, s0_ref,
                o_ref, sf_ref, st_ref, la_ref, og_ref, *, c, nchunk, nlev, nt):
    t = pl.program_id(1)

    @pl.when(t == 0)
    def _():
        st_ref[...] = s0_ref[0]

    zg = _dot(lr_ref[...], wlr_ref[...]) + bg_ref[...]
    la_ref[...] = (-_softplus(-zg) * (1.0 / GLA_TAU)).astype(BF16)

    def head(h, rows):
        kc = slice(h * GLA_DKP, (h + 1) * GLA_DKP)
        vc = slice(h * GLA_DV, (h + 1) * GLA_DV)
        w = jnp.exp(_dot(sum_ref[...], la_ref[rows, kc]))
        qb = q_ref[rows, kc]
        kb = k_ref[rows, kc]
        vb = v_ref[rows, vc]
        q = qb.astype(F32)
        k = kb.astype(F32)
        att = pair_ref[nlev] * _dot_nt(qb, kb)
        for lev in range(nlev):
            wl = w[lev * c:(lev + 1) * c]
            att = att + pair_ref[lev] * _dot_nt((q * wl).astype(BF16), (k * wl).astype(BF16))
        w_pre = w[nlev * c:(nlev + 1) * c]
        w_suf = w[(nlev + 1) * c:(nlev + 2) * c]
        st = st_ref[h]
        og_ref[rows, vc] = _dot(att.astype(BF16), vb) + _dot_nt((q * w_pre).astype(BF16), st.astype(BF16))
        st_ref[h] = st * w_pre[c - 1:c, :] + _dot_tn(vb, (k * w_suf).astype(BF16))

    def chunk(ci, carry):
        rows = pl.ds(pl.multiple_of(ci * c, c), c)
        for h in range(GLA_HEADS):
            head(h, rows)
        return carry

    lax.fori_loop(0, nchunk, chunk, 0, unroll=min(2, nchunk))

    for h in range(GLA_HEADS):
        vc = slice(h * GLA_DV, (h + 1) * GLA_DV)
        o = og_ref[:, vc]
        ms = jnp.mean(o * o, axis=-1, keepdims=True)
        on = o * lax.rsqrt(ms + RMS_EPS) * gn_ref[...]
        g = g_ref[:, vc].astype(F32)
        o_ref[:, vc] = (on * (g * _sigmoid(g))).astype(o_ref.dtype)

    @pl.when(t == nt - 1)
    def _():
        sf_ref[0] = st_ref[...]


def _gla(proj, w_lr2p, b_gatep, g_norm, s0t, *, batch, seq):
    c = CHUNK if seq % CHUNK == 0 else seq
    tt = _tile(seq, 512)
    if tt % c:
        tt = c
    nt = seq // tt
    sums_np, pairs_np, nlev = _gla_tables(c)
    sums = jnp.asarray(sums_np, BF16)
    pairs = jnp.asarray(pairs_np, F32)
    nrow = sums_np.shape[0]
    kw = GLA_HEADS * GLA_DKP
    state_blk = (1, GLA_HEADS, GLA_DV, GLA_DKP)

    in_specs = [
        pl.BlockSpec((tt, MAIN_WIDTH), lambda b, t: (b * nt + t, A_OFF_V // MAIN_WIDTH)),
        pl.BlockSpec((tt, MAIN_WIDTH), lambda b, t: (b * nt + t, A_OFF_G // MAIN_WIDTH)),
        pl.BlockSpec((tt, kw), lambda b, t: (b * nt + t, A_OFF_Q // kw)),
        pl.BlockSpec((tt, kw), lambda b, t: (b * nt + t, A_OFF_K // kw)),
        pl.BlockSpec((tt, GLA_LRP), lambda b, t: (b * nt + t, A_OFF_LR // GLA_LRP)),
        pl.BlockSpec((GLA_LRP, kw), lambda b, t: (0, 0)),
        pl.BlockSpec((1, kw), lambda b, t: (0, 0)),
        pl.BlockSpec((1, GLA_DV), lambda b, t: (0, 0)),
        pl.BlockSpec((nrow, c), lambda b, t: (0, 0)),
        pl.BlockSpec((nlev + 1, c, c), lambda b, t: (0, 0, 0)),
        pl.BlockSpec(state_blk, lambda b, t: (b, 0, 0, 0)),
    ]
    blocks = (3 * _nbytes((tt, MAIN_WIDTH), BF16) + 2 * _nbytes((tt, kw), BF16) + _nbytes((tt, GLA_LRP), BF16)
              + _nbytes((GLA_LRP, kw), BF16) + 2 * _nbytes(state_blk, F32) + _nbytes((nrow, c), BF16)
              + _nbytes((nlev + 1, c, c), F32))
    return pl.pallas_call(
        functools.partial(_gla_kernel, c=c, nchunk=tt // c, nlev=nlev, nt=nt),
        grid=(batch, nt),
        in_specs=in_specs,
        out_specs=[pl.BlockSpec((tt, MAIN_WIDTH), lambda b, t: (b * nt + t, 0)),
                   pl.BlockSpec(state_blk, lambda b, t: (b, 0, 0, 0))],
        out_shape=[jax.ShapeDtypeStruct((batch * seq, MAIN_WIDTH), BF16),
                   jax.ShapeDtypeStruct((batch, GLA_HEADS, GLA_DV, GLA_DKP), F32)],
        scratch_shapes=[pltpu.VMEM(state_blk[1:], F32), pltpu.VMEM((tt, kw), BF16),
                        pltpu.VMEM((tt, MAIN_WIDTH), F32)],
        compiler_params=pltpu.CompilerParams(
            dimension_semantics=("arbitrary", "arbitrary"),
            vmem_limit_bytes=_vmem_limit(
                blocks, temp_bytes=8 << 20,
                scratch_bytes=_nbytes(state_blk, F32) + _nbytes((tt, kw), BF16) + _nbytes((tt, MAIN_WIDTH), F32))),
        name="gla_scan",
    )(proj, proj, proj, proj, proj, w_lr2p, b_gatep, g_norm, sums, pairs, s0t)


def _sb_kernel(q_ref, k_ref, v_ref, u_ref, o_ref, acc_ref, carry_ref, *, tq, tk, q_off, nh):
    qi = pl.program_id(2)
    jd = (q_off + qi * tq) // tk
    q_pos = q_off + qi * tq + lax.broadcasted_iota(jnp.int32, (tq, tk), 0)
    ntile = tk // V7X_LANES

    def tile(h, j, carry, reach):
        cols = slice(h * HEAD_DIM, (h + 1) * HEAD_DIM)
        keys = pl.ds(pl.multiple_of(j * tk, tk), tk)
        z = _dot_nt(q_ref[:, cols], k_ref[keys, cols])
        sp = jnp.maximum(z, 0.0) + jnp.log2(1.0 + jnp.exp2(-jnp.abs(z)))
        if reach is not None:
            sp = jnp.where(reach, sp, 0.0)
        cs = _dot(sp.astype(BF16), u_ref[...])
        w = jnp.exp2(z - cs - jnp.concatenate([carry] * ntile, axis=1))
        if reach is not None:
            w = jnp.where(reach, w, 0.0)
        return _dot(w.astype(BF16), v_ref[keys, cols]), jnp.sum(sp, axis=1, keepdims=True)

    def live(carries):
        low = functools.reduce(jnp.minimum, carries)
        return jnp.min(low, axis=0, keepdims=True)[0, 0] < SB_DEAD_LOG2

    k_col = lax.broadcasted_iota(jnp.int32, (tq, tk), 1)
    diag_reach = (jd * tk + k_col) < q_pos
    has_prev = jnp.where(jd >= 1, 1.0, 0.0)
    carries = []
    for h in range(nh):
        a0, r0 = tile(h, jd, jnp.zeros((tq, V7X_LANES), F32), diag_reach)
        c0 = jnp.broadcast_to(r0, (tq, V7X_LANES))
        a1, r1 = tile(h, jnp.maximum(jd - 1, 0), c0, None)
        c1 = c0 + r1 * has_prev
        acc_ref[h] = a0 + a1 * has_prev
        carry_ref[h] = c1
        carries.append(c1)

    def cond(state):
        j, alive = state
        return jnp.logical_and(j >= 0, alive)

    def body(state):
        j, _ = state
        new = []
        for h in range(nh):
            carry = carry_ref[h]
            a, r = tile(h, j, carry, None)
            acc_ref[h] += a
            carry_ref[h] = carry + r
            new.append(carry + r)
        return j - 1, live(new)

    lax.while_loop(cond, body, (jd - 2, live(carries)))
    for h in range(nh):
        o_ref[:, h * HEAD_DIM:(h + 1) * HEAD_DIM] = acc_ref[h].astype(o_ref.dtype)


def _sb_attention(q_arr, q_col0, k_arr, v_arr, *, batch, seq_q, seq_k, q_off):
    tq = min(SB_TILE, seq_q)
    tk = SB_TILE
    nh = SB_HEADS_PER_STEP
    width = nh * HEAD_DIM
    assert seq_q % tq == 0 and seq_k % tk == 0 and q_off % tk == 0 and tq <= tk
    assert q_col0 % width == 0 and SB_HEADS % nh == 0
    nq = seq_q // tq
    u = jnp.asarray(np.tril(np.ones((tk, tk), np.float32)), BF16)
    blocks = 2 * _nbytes((tq, width), BF16) + 2 * _nbytes((seq_k, width), BF16) + _nbytes((tk, tk), BF16)
    return pl.pallas_call(
        functools.partial(_sb_kernel, tq=tq, tk=tk, q_off=q_off, nh=nh),
        grid=(batch, SB_HEADS // nh, nq),
        in_specs=[pl.BlockSpec((tq, width), lambda b, h, i: (b * nq + i, q_col0 // width + h)),
                  pl.BlockSpec((seq_k, width), lambda b, h, i: (b, h)),
                  pl.BlockSpec((seq_k, width), lambda b, h, i: (b, h)),
                  pl.BlockSpec((tk, tk), lambda b, h, i: (0, 0))],
        out_specs=pl.BlockSpec((tq, width), lambda b, h, i: (b * nq + i, h)),
        out_shape=jax.ShapeDtypeStruct((batch * seq_q, MAIN_WIDTH), BF16),
        scratch_shapes=[pltpu.VMEM((nh, tq, HEAD_DIM), F32), pltpu.VMEM((nh, tq, V7X_LANES), F32)],
        compiler_params=pltpu.CompilerParams(
            dimension_semantics=("arbitrary", "arbitrary", "arbitrary"),
            vmem_limit_bytes=_vmem_limit(blocks, temp_bytes=12 * nh * _nbytes((tq, tk), F32))),
        name="stick_breaking",
    )(q_arr, k_arr, v_arr, u)


def _mem_kernel(qm_ref, mk_ref, mv_ref, o_ref):
    for h in range(MEM_HEADS):
        cols = slice(h * HEAD_DIM, (h + 1) * HEAD_DIM)
        s = _dot_nt(qm_ref[:, cols], mk_ref[0, :, cols])
        e = jnp.exp(s - jnp.max(s, axis=-1, keepdims=True))
        den = jnp.sum(e, axis=-1, keepdims=True)
        o = _dot(e.astype(BF16), mv_ref[0, :, cols]) / den
        o_ref[:, cols] = o.astype(o_ref.dtype)


def _mem_attention(q_arr, q_col0, mk, mv, *, batch, seq):
    tt = _tile(seq, 512)
    nt = seq // tt
    mlen = mk.shape[1]
    assert q_col0 % MEM_WIDTH == 0
    blocks = 2 * _nbytes((tt, MEM_WIDTH), BF16) + 2 * _nbytes((mlen, MEM_WIDTH), BF16)
    return pl.pallas_call(
        _mem_kernel,
        grid=(batch, nt),
        in_specs=[pl.BlockSpec((tt, MEM_WIDTH), lambda b, t: (b * nt + t, q_col0 // MEM_WIDTH)),
                  pl.BlockSpec((1, mlen, MEM_WIDTH), lambda b, t: (b, 0, 0)),
                  pl.BlockSpec((1, mlen, MEM_WIDTH), lambda b, t: (b, 0, 0))],
        out_specs=pl.BlockSpec((tt, MEM_WIDTH), lambda b, t: (b * nt + t, 0)),
        out_shape=jax.ShapeDtypeStruct((batch * seq, MEM_WIDTH), BF16),
        compiler_params=pltpu.CompilerParams(
            dimension_semantics=("arbitrary", "arbitrary"),
            vmem_limit_bytes=_vmem_limit(blocks, temp_bytes=6 * _nbytes((tt, mlen), F32))),
        name="mem_attention",
    )(q_arr, mk, mv)


def _pad_heads(w, width, padded):
    lead = w.shape[:-1]
    w = w.reshape(lead + (GLA_HEADS, width))
    w = jnp.pad(w, [(0, 0)] * len(lead) + [(0, 0), (0, padded - width)])
    return w.reshape(lead + (GLA_HEADS * padded,))


def _layout_in_a(w_in):
    q = w_in[:, :GLA_KW]
    k = w_in[:, GLA_KW:2 * GLA_KW]
    v = w_in[:, 2 * GLA_KW:2 * GLA_KW + MAIN_WIDTH]
    g = w_in[:, 2 * GLA_KW + MAIN_WIDTH:2 * GLA_KW + 2 * MAIN_WIDTH]
    lr = w_in[:, 2 * GLA_KW + 2 * MAIN_WIDTH:2 * GLA_KW + 2 * MAIN_WIDTH + GLA_LOWRANK]
    qm = w_in[:, 2 * GLA_KW + 2 * MAIN_WIDTH + GLA_LOWRANK:]
    lr = jnp.pad(lr, ((0, 0), (0, GLA_LRP - GLA_LOWRANK)))
    w = jnp.concatenate([v, g, _pad_heads(q, GLA_DK, GLA_DKP), _pad_heads(k, GLA_DK, GLA_DKP), qm, lr], axis=1)
    scale = np.ones((1, A_WIDTH), np.float32)
    scale[:, A_OFF_Q:A_OFF_K] = GLA_DK ** -0.5
    scale[:, A_OFF_QM:A_OFF_LR] = MEM_SCALE
    return w.astype(BF16), jnp.asarray(scale)


def _scale_in_b():
    scale = np.ones((1, D_MODEL), np.float32)
    scale[:, :MAIN_WIDTH] = SB_SCALE * LOG2_E
    scale[:, MAIN_WIDTH:] = MEM_SCALE
    return jnp.asarray(scale)


def _finish_layer(x, o_main, o_mem, lw):
    x1, x1b = _matmul_residual_ln([(o_main, lw["w_o_main"]), (o_mem, lw["w_o_mem"])], x,
                                  lw["ln1_g"], lw["ln1_b"], tm=512)
    hid = _ffn_in(x1b, lw["w_ffn_in"])
    return _matmul_residual_ln([(hid, lw["w_ffn_out"])], x1, lw["ln2_g"], lw["ln2_b"], tm=256)


def _layer_a(x, xb, lw, mk, mv, s0t, *, batch, seq):
    proj, = _matmul(xb, lw["w_in"], out_dtypes=(BF16,), col_scale=lw["in_scale"], tm=1024, tn=1920)
    o_main, sft = _gla(proj, lw["w_lr2p"], lw["b_gatep"], lw["g_norm"], s0t, batch=batch, seq=seq)
    o_mem = _mem_attention(proj, A_OFF_QM, mk, mv, batch=batch, seq=seq)
    x, xb = _finish_layer(x, o_main, o_mem, lw)
    return x, xb, sft


def _layer_b(x, xb, lw, mk, mv, kb, vb, *, batch, seq, seq_k, q_off):
    proj, = _matmul(xb, lw["w_in"], out_dtypes=(BF16,), col_scale=lw["in_scale"], tm=1024, tn=1024)
    o_main = _sb_attention(proj, 0, kb, vb, batch=batch, seq_q=seq, seq_k=seq_k, q_off=q_off)
    o_mem = _mem_attention(proj, MAIN_WIDTH, mk, mv, batch=batch, seq=seq)
    return _finish_layer(x, o_main, o_mem, lw)


def _state_to_kernel(s):
    st = jnp.swapaxes(s, -1, -2)
    return jnp.pad(st, ((0, 0), (0, 0), (0, 0), (0, GLA_DKP - GLA_DK)))


def _state_from_kernel(st):
    return jnp.swapaxes(st[..., :GLA_DK], -1, -2)


def kernel(x_prompt, x_sample, mem_prompt, state_gla, cache_sb_k, cache_sb_v, cache_mem_k, cache_mem_v,
           w_in_a, w_gate_lr, b_gate, gla_norm_g, w_in_b, w_kv_shared, w_mem_kv, w_o, ln1_g, ln1_b,
           ln2_g, ln2_b, w_ffn_in, w_ffn_out):
    bp, lp, _ = x_prompt.shape
    bs, ls, _ = x_sample.shape
    mlen = mem_prompt.shape[1]

    layers = []
    for l in range(DEPTH):
        lw = {
            "w_o_main": w_o[l, :MAIN_WIDTH].astype(BF16),
            "w_o_mem": w_o[l, MAIN_WIDTH:].astype(BF16),
            "ln1_g": ln1_g[l][None, :], "ln1_b": ln1_b[l][None, :],
            "ln2_g": ln2_g[l][None, :], "ln2_b": ln2_b[l][None, :],
            "w_ffn_in": w_ffn_in[l].astype(BF16),
            "w_ffn_out": w_ffn_out[l].astype(BF16),
        }
        if l < N_A:
            lw["w_in"], lw["in_scale"] = _layout_in_a(w_in_a[l])
            lr2 = _pad_heads(w_gate_lr[l], GLA_DK, GLA_DKP)
            lw["w_lr2p"] = jnp.pad(lr2, ((0, GLA_LRP - GLA_LOWRANK), (0, 0))).astype(BF16)
            lw["b_gatep"] = _pad_heads(b_gate[l][None, :], GLA_DK, GLA_DKP)
            lw["g_norm"] = gla_norm_g[l][None, :]
        else:
            lw["w_in"], lw["in_scale"] = w_in_b[l - N_A].astype(BF16), _scale_in_b()
        layers.append(lw)
    w_kv_b = w_kv_shared.astype(BF16)
    w_memkv_b = w_mem_kv.astype(BF16)

    xp = x_prompt.reshape(bp * lp, D_MODEL)
    xs = x_sample.reshape(bs * ls, D_MODEL)
    xpb, xsb = xp, xs
    memb = mem_prompt.reshape(bp * mlen, D_MODEL).astype(BF16)
    cmk = cache_mem_k.reshape(DEPTH, bs, mlen, MEM_WIDTH).astype(BF16)
    cmv = cache_mem_v.reshape(DEPTH, bs, mlen, MEM_WIDTH).astype(BF16)

    s_zero = jnp.zeros((bp, GLA_HEADS, GLA_DV, GLA_DKP), F32)
    gla_p, gla_s, mk_list, mv_list = [], [], [], []
    for l in range(DEPTH):
        lw = layers[l]
        mkv, = _matmul(memb, w_memkv_b[l], out_dtypes=(F32,), tm=1024, tn=1024)
        mk_p, mv_p = mkv[:, :MEM_WIDTH], mkv[:, MEM_WIDTH:]
        mk_list.append(mk_p.reshape(bp, mlen, MEM_HEADS, HEAD_DIM))
        mv_list.append(mv_p.reshape(bp, mlen, MEM_HEADS, HEAD_DIM))
        mk_pb = mk_p.astype(BF16).reshape(bp, mlen, MEM_WIDTH)
        mv_pb = mv_p.astype(BF16).reshape(bp, mlen, MEM_WIDTH)
        if l < N_A:
            xp, xpb, sp = _layer_a(xp, xpb, lw, mk_pb, mv_pb, s_zero, batch=bp, seq=lp)
            xs, xsb, ss = _layer_a(xs, xsb, lw, cmk[l], cmv[l], _state_to_kernel(state_gla[l]),
                                   batch=bs, seq=ls)
            gla_p.append(_state_from_kernel(sp))
            gla_s.append(_state_from_kernel(ss))
        else:
            if l == N_A:
                half = MAIN_WIDTH
                kp_f, kp_b = _matmul(xpb, w_kv_b[:, :half], out_dtypes=(F32, BF16), tm=1024, tn=768)
                vp_f, vp_b = _matmul(xpb, w_kv_b[:, half:], out_dtypes=(F32, BF16), tm=1024, tn=768)
                ks_f, ks_b = _matmul(xsb, w_kv_b[:, :half], out_dtypes=(F32, BF16), tm=1024, tn=768)
                vs_f, vs_b = _matmul(xsb, w_kv_b[:, half:], out_dtypes=(F32, BF16), tm=1024, tn=768)
                lk = PAST_LEN + ls
                lk_pad = -(-lk // SB_TILE) * SB_TILE

                def with_cache(cache, new_b):
                    allk = jnp.concatenate([cache.reshape(bs, PAST_LEN, MAIN_WIDTH).astype(BF16),
                                            new_b.reshape(bs, ls, MAIN_WIDTH)], axis=1)
                    allk = jnp.pad(allk, ((0, 0), (0, lk_pad - lk), (0, 0)))
                    return allk.reshape(bs * lk_pad, MAIN_WIDTH)

                k_all_b = with_cache(cache_sb_k, ks_b)
                v_all_b = with_cache(cache_sb_v, vs_b)
            xp, xpb = _layer_b(xp, xpb, lw, mk_pb, mv_pb, kp_b, vp_b, batch=bp, seq=lp, seq_k=lp, q_off=0)
            xs, xsb = _layer_b(xs, xsb, lw, cmk[l], cmv[l], k_all_b, v_all_b, batch=bs, seq=ls,
                               seq_k=lk_pad, q_off=PAST_LEN)

    return (xp.reshape(bp, lp, D_MODEL), xs.reshape(bs, ls, D_MODEL),
            jnp.stack(gla_p, 0), jnp.stack(gla_s, 0),
            kp_f.reshape(bp, lp, SB_HEADS, HEAD_DIM), vp_f.reshape(bp, lp, SB_HEADS, HEAD_DIM),
            ks_f.reshape(bs, ls, SB_HEADS, HEAD_DIM), vs_f.reshape(bs, ls, SB_HEADS, HEAD_DIM),
            jnp.stack(mk_list, 0), jnp.stack(mv_list, 0))
```

```python
import functools
import math

import numpy as np
import jax
import jax.numpy as jnp
from jax import lax
from jax.experimental import pallas as pl
from jax.experimental.pallas import tpu as pltpu

D_MODEL = 2048
DEPTH = 4
PAST_LEN = 2048
CHUNK = 64
N_A = DEPTH // 2
HEAD_DIM = 128
MEM_HEADS = 4
MEM_WIDTH = MEM_HEADS * HEAD_DIM
MAIN_WIDTH = D_MODEL - MEM_WIDTH
GLA_HEADS = 4
GLA_DV = MAIN_WIDTH // GLA_HEADS
GLA_DK = GLA_DV // 2
GLA_KW = GLA_HEADS * GLA_DK
GLA_LOWRANK = 16
GLA_TAU = 16.0
SB_HEADS = MAIN_WIDTH // HEAD_DIM
D_FF = ((8 * D_MODEL // 3 + 255) // 256) * 256
DN_ALPHA = (2 * DEPTH) ** 0.25
LN_EPS = 1e-5
RMS_EPS = 1e-6
SB_SCALE = HEAD_DIM ** -0.5
MEM_SCALE = HEAD_DIM ** -0.5

V7X_LANES = 128
V7X_MXU_DIM = 256
V7X_VMEM_BYTES = 64 * 1024 * 1024

GLA_DKP = V7X_MXU_DIM
GLA_LRP = V7X_LANES
A_OFF_V = 0
A_OFF_G = MAIN_WIDTH
A_OFF_Q = 2 * MAIN_WIDTH
A_OFF_K = A_OFF_Q + GLA_HEADS * GLA_DKP
A_OFF_QM = A_OFF_K + GLA_HEADS * GLA_DKP
A_OFF_LR = A_OFF_QM + MEM_WIDTH
A_WIDTH = A_OFF_LR + GLA_LRP

SB_TILE = 256
SB_HEADS_PER_STEP = 4
SB_DEAD_LOG2 = 150.0
LOG2_E = math.log2(math.e)

F32 = jnp.float32
BF16 = jnp.bfloat16


def _vmem_limit(block_bytes, scratch_bytes=0, temp_bytes=0):
    need = 2 * block_bytes + scratch_bytes + temp_bytes + (2 << 20)
    return int(min(max(need, 16 << 20), V7X_VMEM_BYTES - (8 << 20)))


def _nbytes(shape, dtype):
    return int(np.prod(shape)) * jnp.dtype(dtype).itemsize


def _tile(n, pref):
    if n <= pref:
        return n
    t = pref
    while n % t:
        t -= 1
    return t


def _dot(a, b):
    return jnp.dot(a, b, preferred_element_type=F32)


def _dot_nt(a, b):
    return lax.dot_general(a, b, (((1,), (1,)), ((), ())), preferred_element_type=F32)


def _dot_tn(a, b):
    return lax.dot_general(a, b, (((0,), (0,)), ((), ())), preferred_element_type=F32)


def _sigmoid(x):
    return 1.0 / (1.0 + jnp.exp(-x))


def _softplus(x):
    return jnp.maximum(x, 0.0) + jnp.log(1.0 + jnp.exp(-jnp.abs(x)))


def _mm_kernel(*refs, has_scale):
    if has_scale:
        a_ref, w_ref, s_ref = refs[:3]
        o_refs = refs[3:]
    else:
        a_ref, w_ref = refs[:2]
        o_refs = refs[2:]
    acc = _dot(a_ref[...].astype(BF16), w_ref[...])
    if has_scale:
        acc = acc * s_ref[...]
    for o_ref in o_refs:
        o_ref[...] = acc.astype(o_ref.dtype)


def _weight_spec(w, block, index_map, layer, **kwargs):
    if w.ndim == 2:
        return pl.BlockSpec(block, index_map, **kwargs)
    return pl.BlockSpec((None,) + block, lambda *g: (layer,) + index_map(*g), **kwargs)


def _matmul(a, w, *, out_dtypes, layer=None, col_scale=None, tm=1024, tn=1024):
    m, k = a.shape
    n = w.shape[-1]
    tm, tn = _tile(m, tm), _tile(n, tn)
    in_specs = [pl.BlockSpec((tm, k), lambda j, i: (i, 0)),
                _weight_spec(w, (k, tn), lambda j, i: (0, j), layer)]
    args = [a, w]
    if col_scale is not None:
        in_specs.append(pl.BlockSpec((1, tn), lambda j, i: (0, j)))
        args.append(col_scale)
    blocks = _nbytes((tm, k), a.dtype) + _nbytes((k, tn), BF16) + sum(_nbytes((tm, tn), d) for d in out_dtypes)
    outs = pl.pallas_call(
        functools.partial(_mm_kernel, has_scale=col_scale is not None),
        grid=(n // tn, m // tm),
        in_specs=in_specs,
        out_specs=[pl.BlockSpec((tm, tn), lambda j, i: (i, j)) for _ in out_dtypes],
        out_shape=[jax.ShapeDtypeStruct((m, n), d) for d in out_dtypes],
        compiler_params=pltpu.CompilerParams(
            dimension_semantics=("arbitrary", "arbitrary"),
            vmem_limit_bytes=_vmem_limit(blocks, temp_bytes=2 * _nbytes((tm, tn), F32))),
        name="proj_matmul",
    )(*args)
    return outs


def _swiglu_kernel(a_ref, wg_ref, wu_ref, o_ref):
    a = a_ref[...]
    gate = _dot(a, wg_ref[...])
    up = _dot(a, wu_ref[...])
    o_ref[...] = (gate * _sigmoid(gate) * up).astype(o_ref.dtype)


def _ffn_in(a, w, layer, *, tm=2048, tn=512):
    m, k = a.shape
    tm, tn = _tile(m, tm), _tile(D_FF, tn)
    nj = D_FF // tn
    blocks = _nbytes((tm, k), BF16) + 2 * _nbytes((k, tn), BF16) + _nbytes((tm, tn), BF16)
    return pl.pallas_call(
        _swiglu_kernel,
        grid=(nj, m // tm),
        in_specs=[pl.BlockSpec((tm, k), lambda j, i: (i, 0)),
                  _weight_spec(w, (k, tn), lambda j, i: (0, j), layer),
                  _weight_spec(w, (k, tn), lambda j, i: (0, j + nj), layer)],
        out_specs=pl.BlockSpec((tm, tn), lambda j, i: (i, j)),
        out_shape=jax.ShapeDtypeStruct((m, D_FF), BF16),
        compiler_params=pltpu.CompilerParams(
            dimension_semantics=("arbitrary", "arbitrary"),
            vmem_limit_bytes=_vmem_limit(blocks, temp_bytes=4 * _nbytes((tm, tn), F32))),
        name="ffn_in_swiglu",
    )(a, w, w)


def _ln_kernel(*refs, npair, nsplit):
    aw_refs = refs[:2 * npair]
    x_ref, g_ref, b_ref, of_ref, ob_ref = refs[2 * npair:]
    rows_per = x_ref.shape[0] // nsplit
    for r in range(nsplit):
        rows = slice(r * rows_per, (r + 1) * rows_per)
        acc = _dot(aw_refs[0][rows, :], aw_refs[1][...])
        for p in range(1, npair):
            acc = acc + _dot(aw_refs[2 * p][rows, :], aw_refs[2 * p + 1][...])
        y = DN_ALPHA * x_ref[rows, :] + acc
        mu = jnp.mean(y, axis=-1, keepdims=True)
        d = y - mu
        var = jnp.mean(d * d, axis=-1, keepdims=True)
        out = d * lax.rsqrt(var + LN_EPS) * g_ref[...] + b_ref[...]
        of_ref[rows, :] = out
        ob_ref[rows, :] = out.astype(ob_ref.dtype)


def _matmul_residual_ln(pairs, x, gamma, beta, *, tm, nsplit=1):
    m, n = x.shape
    tm = _tile(m, tm)
    assert tm % nsplit == 0
    in_specs, args, blocks, resident = [], [], 0, 0
    for a, w, layer, row_blk in pairs:
        ka = a.shape[1]
        in_specs += [pl.BlockSpec((tm, ka), lambda i: (i, 0)),
                     _weight_spec(w, (ka, n), lambda i, rb=row_blk: (rb, 0), layer,
                                  pipeline_mode=pl.Buffered(1))]
        args += [a, w]
        blocks += _nbytes((tm, ka), BF16)
        resident += _nbytes((ka, n), BF16)
    in_specs += [pl.BlockSpec((tm, n), lambda i: (i, 0)),
                 pl.BlockSpec((1, n), lambda i: (0, 0)),
                 pl.BlockSpec((1, n), lambda i: (0, 0))]
    blocks += 2 * _nbytes((tm, n), F32) + _nbytes((tm, n), BF16)
    return pl.pallas_call(
        functools.partial(_ln_kernel, npair=len(pairs), nsplit=nsplit),
        grid=(m // tm,),
        in_specs=in_specs,
        out_specs=[pl.BlockSpec((tm, n), lambda i: (i, 0)),
                   pl.BlockSpec((tm, n), lambda i: (i, 0))],
        out_shape=[jax.ShapeDtypeStruct((m, n), F32), jax.ShapeDtypeStruct((m, n), BF16)],
        compiler_params=pltpu.CompilerParams(
            dimension_semantics=("arbitrary",),
            vmem_limit_bytes=_vmem_limit(blocks, scratch_bytes=resident,
                                         temp_bytes=3 * _nbytes((tm, n), F32))),
        name="matmul_residual_ln",
    )(*args, x, gamma, beta)


def _kv_kernel(a_ref, w_ref, of_ref, ob_ref):
    acc = _dot(a_ref[...], w_ref[...])
    ob_ref[...] = acc.astype(ob_ref.dtype)
    for h in range(SB_HEADS):
        of_ref[0, :, h, :] = acc[:, h * HEAD_DIM:(h + 1) * HEAD_DIM]


def _kv_proj(a, w, col_blk, *, batch, seq, tm=512):
    m, k = a.shape
    tm = _tile(seq, tm)
    nb = seq // tm
    sublane_padded_heads = -(-SB_HEADS // 8) * 8
    blocks = (_nbytes((tm, k), BF16) + _nbytes((tm, MAIN_WIDTH), BF16)
              + _nbytes((tm, sublane_padded_heads, HEAD_DIM), F32))
    return pl.pallas_call(
        _kv_kernel,
        grid=(m // tm,),
        in_specs=[pl.BlockSpec((tm, k), lambda i: (i, 0)),
                  pl.BlockSpec((k, MAIN_WIDTH), lambda i: (0, col_blk), pipeline_mode=pl.Buffered(1))],
        out_specs=[pl.BlockSpec((1, tm, SB_HEADS, HEAD_DIM), lambda i: (i // nb, i % nb, 0, 0)),
                   pl.BlockSpec((tm, MAIN_WIDTH), lambda i: (i, 0))],
        out_shape=[jax.ShapeDtypeStruct((batch, seq, SB_HEADS, HEAD_DIM), F32),
                   jax.ShapeDtypeStruct((m, MAIN_WIDTH), BF16)],
        compiler_params=pltpu.CompilerParams(
            dimension_semantics=("arbitrary",),
            vmem_limit_bytes=_vmem_limit(blocks, scratch_bytes=_nbytes((k, MAIN_WIDTH), BF16),
                                         temp_bytes=2 * _nbytes((tm, MAIN_WIDTH), F32))),
        name="kv_proj",
    )(a, w)


def _gla_tables(c):
    nlev = int(math.log2(c))
    assert 1 << nlev == c
    idx = np.arange(c)
    i, t = idx[:, None], idx[None, :]
    sums, pairs = [], []
    for lev in range(nlev):
        h = c >> (lev + 1)
        p = (i // (2 * h)) * 2 * h + h - 1
        second = (i % (2 * h)) >= h
        sums.append(np.where(second, (t > p) & (t <= i), (t > i) & (t <= p)))
        pairs.append((i // (2 * h) == t // (2 * h)) & second & ((t % (2 * h)) < h))
    sums.append(t <= i)
    sums.append(t > i)
    pairs.append(i == t)
    return (np.concatenate(sums, 0).astype(np.float32), np.stack(pairs, 0).astype(np.float32), nlev)


def _gla_kernel(v_ref, g_ref, q_ref, k_ref, lr_ref, wlr_ref, bg_ref, gn_ref, sum_ref, pair_ref, s0_ref,
                o_ref, sf_ref, st_ref, la_ref, og_ref, *, c, nchunk, nlev, nt):
    t = pl.program_id(1)

    @pl.when(t == 0)
    def _():
        pad = jnp.zeros((GLA_DKP - GLA_DK, GLA_DV), F32)
        for h in range(GLA_HEADS):
            st_ref[h] = jnp.concatenate([s0_ref[0, h], pad], axis=0).T

    zg = _dot(lr_ref[...], wlr_ref[...]) + bg_ref[...]
    la_ref[...] = (-_softplus(-zg) * (1.0 / GLA_TAU)).astype(BF16)

    def head(h, rows):
        kc = slice(h * GLA_DKP, (h + 1) * GLA_DKP)
        vc = slice(h * GLA_DV, (h + 1) * GLA_DV)
        w = jnp.exp(_dot(sum_ref[...], la_ref[rows, kc]))
        qb = q_ref[rows, kc]
        kb = k_ref[rows, kc]
        vb = v_ref[rows, vc]
        q = qb.astype(F32)
        k = kb.astype(F32)
        att = pair_ref[nlev] * _dot_nt(qb, kb)
        for lev in range(nlev):
            wl = w[lev * c:(lev + 1) * c]
            att = att + pair_ref[lev] * _dot_nt((q * wl).astype(BF16), (k * wl).astype(BF16))
        w_pre = w[nlev * c:(nlev + 1) * c]
        w_suf = w[(nlev + 1) * c:(nlev + 2) * c]
        st = st_ref[h]
        og_ref[rows, vc] = _dot(att.astype(BF16), vb) + _dot_nt((q * w_pre).astype(BF16), st.astype(BF16))
        st_ref[h] = st * w_pre[c - 1:c, :] + _dot_tn(vb, (k * w_suf).astype(BF16))

    def chunk(ci, carry):
        rows = pl.ds(pl.multiple_of(ci * c, c), c)
        for h in range(GLA_HEADS):
            head(h, rows)
        return carry

    lax.fori_loop(0, nchunk, chunk, 0, unroll=min(2, nchunk))

    for h in range(GLA_HEADS):
        vc = slice(h * GLA_DV, (h + 1) * GLA_DV)
        o = og_ref[:, vc]
        ms = jnp.mean(o * o, axis=-1, keepdims=True)
        on = o * lax.rsqrt(ms + RMS_EPS) * gn_ref[...]
        g = g_ref[:, vc].astype(F32)
        o_ref[:, vc] = (on * (g * _sigmoid(g))).astype(o_ref.dtype)

    @pl.when(t == nt - 1)
    def _():
        for h in range(GLA_HEADS):
            sf_ref[0, h] = st_ref[h].T[:GLA_DK, :]


def _gla(proj, w_lr2p, b_gatep, g_norm, s0, *, batch, seq):
    c = CHUNK if seq % CHUNK == 0 else seq
    tt = _tile(seq, 512)
    if tt % c:
        tt = c
    nt = seq // tt
    sums_np, pairs_np, nlev = _gla_tables(c)
    sums = jnp.asarray(sums_np, BF16)
    pairs = jnp.asarray(pairs_np, F32)
    nrow = sums_np.shape[0]
    kw = GLA_HEADS * GLA_DKP
    state_blk = (1, GLA_HEADS, GLA_DK, GLA_DV)
    state_vmem = (GLA_HEADS, GLA_DV, GLA_DKP)

    in_specs = [
        pl.BlockSpec((tt, MAIN_WIDTH), lambda b, t: (b * nt + t, A_OFF_V // MAIN_WIDTH)),
        pl.BlockSpec((tt, MAIN_WIDTH), lambda b, t: (b * nt + t, A_OFF_G // MAIN_WIDTH)),
        pl.BlockSpec((tt, kw), lambda b, t: (b * nt + t, A_OFF_Q // kw)),
        pl.BlockSpec((tt, kw), lambda b, t: (b * nt + t, A_OFF_K // kw)),
        pl.BlockSpec((tt, GLA_LRP), lambda b, t: (b * nt + t, A_OFF_LR // GLA_LRP)),
        pl.BlockSpec((GLA_LRP, kw), lambda b, t: (0, 0)),
        pl.BlockSpec((1, kw), lambda b, t: (0, 0)),
        pl.BlockSpec((1, GLA_DV), lambda b, t: (0, 0)),
        pl.BlockSpec((nrow, c), lambda b, t: (0, 0)),
        pl.BlockSpec((nlev + 1, c, c), lambda b, t: (0, 0, 0)),
        pl.BlockSpec(state_blk, lambda b, t: (b, 0, 0, 0)),
    ]
    blocks = (3 * _nbytes((tt, MAIN_WIDTH), BF16) + 2 * _nbytes((tt, kw), BF16) + _nbytes((tt, GLA_LRP), BF16)
              + _nbytes((GLA_LRP, kw), BF16) + 2 * _nbytes(state_blk, F32) + _nbytes((nrow, c), BF16)
              + _nbytes((nlev + 1, c, c), F32))
    return pl.pallas_call(
        functools.partial(_gla_kernel, c=c, nchunk=tt // c, nlev=nlev, nt=nt),
        grid=(batch, nt),
        in_specs=in_specs,
        out_specs=[pl.BlockSpec((tt, MAIN_WIDTH), lambda b, t: (b * nt + t, 0)),
                   pl.BlockSpec(state_blk, lambda b, t: (b, 0, 0, 0))],
        out_shape=[jax.ShapeDtypeStruct((batch * seq, MAIN_WIDTH), BF16),
                   jax.ShapeDtypeStruct((batch, GLA_HEADS, GLA_DK, GLA_DV), F32)],
        scratch_shapes=[pltpu.VMEM(state_vmem, F32), pltpu.VMEM((tt, kw), BF16),
                        pltpu.VMEM((tt, MAIN_WIDTH), F32)],
        compiler_params=pltpu.CompilerParams(
            dimension_semantics=("arbitrary", "arbitrary"),
            vmem_limit_bytes=_vmem_limit(
                blocks, temp_bytes=8 << 20,
                scratch_bytes=_nbytes(state_vmem, F32) + _nbytes((tt, kw), BF16) + _nbytes((tt, MAIN_WIDTH), F32))),
        name="gla_scan",
    )(proj, proj, proj, proj, proj, w_lr2p, b_gatep, g_norm, sums, pairs, s0)


def _sb_kernel(q_ref, k_ref, v_ref, u_ref, o_ref, acc_ref, carry_ref, *, tq, tk, q_off, nh):
    qi = pl.program_id(2)
    jd = (q_off + qi * tq) // tk
    q_pos = q_off + qi * tq + lax.broadcasted_iota(jnp.int32, (tq, tk), 0)
    ntile = tk // V7X_LANES

    def tile(h, j, carry, reach):
        cols = slice(h * HEAD_DIM, (h + 1) * HEAD_DIM)
        keys = pl.ds(pl.multiple_of(j * tk, tk), tk)
        z = _dot_nt(q_ref[:, cols], k_ref[keys, cols])
        sp = jnp.maximum(z, 0.0) + jnp.log2(1.0 + jnp.exp2(-jnp.abs(z)))
        if reach is not None:
            sp = jnp.where(reach, sp, 0.0)
        cs = _dot(sp.astype(BF16), u_ref[...])
        w = jnp.exp2(z - cs - jnp.concatenate([carry] * ntile, axis=1))
        if reach is not None:
            w = jnp.where(reach, w, 0.0)
        return _dot(w.astype(BF16), v_ref[keys, cols]), jnp.sum(sp, axis=1, keepdims=True)

    def live(carries):
        low = functools.reduce(jnp.minimum, carries)
        return jnp.min(low, axis=0, keepdims=True)[0, 0] < SB_DEAD_LOG2

    k_col = lax.broadcasted_iota(jnp.int32, (tq, tk), 1)
    diag_reach = (jd * tk + k_col) < q_pos
    has_prev = jnp.where(jd >= 1, 1.0, 0.0)
    carries = []
    for h in range(nh):
        a0, r0 = tile(h, jd, jnp.zeros((tq, V7X_LANES), F32), diag_reach)
        c0 = jnp.broadcast_to(r0, (tq, V7X_LANES))
        a1, r1 = tile(h, jnp.maximum(jd - 1, 0), c0, None)
        c1 = c0 + r1 * has_prev
        acc_ref[h] = a0 + a1 * has_prev
        carry_ref[h] = c1
        carries.append(c1)

    def cond(state):
        j, alive = state
        return jnp.logical_and(j >= 0, alive)

    def body(state):
        j, _ = state
        new = []
        for h in range(nh):
            carry = carry_ref[h]
            a, r = tile(h, j, carry, None)
            acc_ref[h] += a
            carry_ref[h] = carry + r
            new.append(carry + r)
        return j - 1, live(new)

    lax.while_loop(cond, body, (jd - 2, live(carries)))
    for h in range(nh):
        o_ref[:, h * HEAD_DIM:(h + 1) * HEAD_DIM] = acc_ref[h].astype(o_ref.dtype)


def _sb_attention(q_arr, q_col0, k_arr, v_arr, *, batch, seq_q, seq_k, q_off):
    tq = min(SB_TILE, seq_q)
    tk = SB_TILE
    nh = SB_HEADS_PER_STEP
    width = nh * HEAD_DIM
    assert seq_q % tq == 0 and seq_k % tk == 0 and q_off % tk == 0 and tq <= tk
    assert q_col0 % width == 0 and SB_HEADS % nh == 0
    nq = seq_q // tq
    u = jnp.asarray(np.tril(np.ones((tk, tk), np.float32)), BF16)
    blocks = 2 * _nbytes((tq, width), BF16) + 2 * _nbytes((seq_k, width), BF16) + _nbytes((tk, tk), BF16)
    return pl.pallas_call(
        functools.partial(_sb_kernel, tq=tq, tk=tk, q_off=q_off, nh=nh),
        grid=(batch, SB_HEADS // nh, nq),
        in_specs=[pl.BlockSpec((tq, width), lambda b, h, i: (b * nq + i, q_col0 // width + h)),
                  pl.BlockSpec((seq_k, width), lambda b, h, i: (b, h)),
                  pl.BlockSpec((seq_k, width), lambda b, h, i: (b, h)),
                  pl.BlockSpec((tk, tk), lambda b, h, i: (0, 0))],
        out_specs=pl.BlockSpec((tq, width), lambda b, h, i: (b * nq + i, h)),
        out_shape=jax.ShapeDtypeStruct((batch * seq_q, MAIN_WIDTH), BF16),
        scratch_shapes=[pltpu.VMEM((nh, tq, HEAD_DIM), F32), pltpu.VMEM((nh, tq, V7X_LANES), F32)],
        compiler_params=pltpu.CompilerParams(
            dimension_semantics=("arbitrary", "arbitrary", "arbitrary"),
            vmem_limit_bytes=_vmem_limit(blocks, temp_bytes=12 * nh * _nbytes((tq, tk), F32))),
        name="stick_breaking",
    )(q_arr, k_arr, v_arr, u)


def _mem_kernel(qm_ref, mk_ref, mv_ref, o_ref):
    for h in range(MEM_HEADS):
        cols = slice(h * HEAD_DIM, (h + 1) * HEAD_DIM)
        s = _dot_nt(qm_ref[:, cols], mk_ref[0, :, cols])
        e = jnp.exp(s - jnp.max(s, axis=-1, keepdims=True))
        den = jnp.sum(e, axis=-1, keepdims=True)
        o = _dot(e.astype(BF16), mv_ref[0, :, cols]) / den
        o_ref[:, cols] = o.astype(o_ref.dtype)


def _mem_attention(q_arr, q_col0, mk, mv, *, batch, seq):
    tt = _tile(seq, 512)
    nt = seq // tt
    mlen = mk.shape[1]
    assert q_col0 % MEM_WIDTH == 0
    blocks = 2 * _nbytes((tt, MEM_WIDTH), BF16) + 2 * _nbytes((mlen, MEM_WIDTH), BF16)
    return pl.pallas_call(
        _mem_kernel,
        grid=(batch, nt),
        in_specs=[pl.BlockSpec((tt, MEM_WIDTH), lambda b, t: (b * nt + t, q_col0 // MEM_WIDTH)),
                  pl.BlockSpec((1, mlen, MEM_WIDTH), lambda b, t: (b, 0, 0)),
                  pl.BlockSpec((1, mlen, MEM_WIDTH), lambda b, t: (b, 0, 0))],
        out_specs=pl.BlockSpec((tt, MEM_WIDTH), lambda b, t: (b * nt + t, 0)),
        out_shape=jax.ShapeDtypeStruct((batch * seq, MEM_WIDTH), BF16),
        compiler_params=pltpu.CompilerParams(
            dimension_semantics=("arbitrary", "arbitrary"),
            vmem_limit_bytes=_vmem_limit(blocks, temp_bytes=6 * _nbytes((tt, mlen), F32))),
        name="mem_attention",
    )(q_arr, mk, mv)


def _pad_heads(w, width, padded):
    lead = w.shape[:-1]
    w = w.reshape(lead + (GLA_HEADS, width))
    w = jnp.pad(w, [(0, 0)] * len(lead) + [(0, 0), (0, padded - width)])
    return w.reshape(lead + (GLA_HEADS * padded,))


def _layout_in_a(w_in):
    q = w_in[:, :GLA_KW]
    k = w_in[:, GLA_KW:2 * GLA_KW]
    v = w_in[:, 2 * GLA_KW:2 * GLA_KW + MAIN_WIDTH]
    g = w_in[:, 2 * GLA_KW + MAIN_WIDTH:2 * GLA_KW + 2 * MAIN_WIDTH]
    lr = w_in[:, 2 * GLA_KW + 2 * MAIN_WIDTH:2 * GLA_KW + 2 * MAIN_WIDTH + GLA_LOWRANK]
    qm = w_in[:, 2 * GLA_KW + 2 * MAIN_WIDTH + GLA_LOWRANK:]
    lr = jnp.pad(lr, ((0, 0), (0, GLA_LRP - GLA_LOWRANK)))
    w = jnp.concatenate([v, g, _pad_heads(q, GLA_DK, GLA_DKP), _pad_heads(k, GLA_DK, GLA_DKP), qm, lr], axis=1)
    scale = np.ones((1, A_WIDTH), np.float32)
    scale[:, A_OFF_Q:A_OFF_K] = GLA_DK ** -0.5
    scale[:, A_OFF_QM:A_OFF_LR] = MEM_SCALE
    return w.astype(BF16), jnp.asarray(scale)


def _scale_in_b():
    scale = np.ones((1, D_MODEL), np.float32)
    scale[:, :MAIN_WIDTH] = SB_SCALE * LOG2_E
    scale[:, MAIN_WIDTH:] = MEM_SCALE
    return jnp.asarray(scale)


def _finish_layer(x, o_main, o_mem, lw, shared):
    l = lw["layer"]
    w_o = shared["w_o"]
    mem_row_blk = MAIN_WIDTH // MEM_WIDTH
    x1, x1b = _matmul_residual_ln([(o_main, w_o, l, 0), (o_mem, w_o, l, mem_row_blk)], x,
                                  lw["ln1_g"], lw["ln1_b"], tm=512, nsplit=2)
    hid = _ffn_in(x1b, shared["w_ffn_in"], l)
    return _matmul_residual_ln([(hid, shared["w_ffn_out"], l, 0)], x1, lw["ln2_g"], lw["ln2_b"], tm=256)


def _layer_a(x, xb, lw, shared, mk, mv, s0, *, batch, seq):
    proj, = _matmul(xb, lw["w_in"], out_dtypes=(BF16,), col_scale=lw["in_scale"], tm=1024, tn=1920)
    o_main, s_final = _gla(proj, lw["w_lr2p"], lw["b_gatep"], lw["g_norm"], s0, batch=batch, seq=seq)
    o_mem = _mem_attention(proj, A_OFF_QM, mk, mv, batch=batch, seq=seq)
    x, xb = _finish_layer(x, o_main, o_mem, lw, shared)
    return x, xb, s_final


def _layer_b(x, xb, lw, shared, mk, mv, kb, vb, *, batch, seq, seq_k, q_off):
    proj, = _matmul(xb, shared["w_in_b"], layer=lw["layer"] - N_A, out_dtypes=(BF16,),
                    col_scale=shared["in_b_scale"], tm=1024, tn=1024)
    o_main = _sb_attention(proj, 0, kb, vb, batch=batch, seq_q=seq, seq_k=seq_k, q_off=q_off)
    o_mem = _mem_attention(proj, MAIN_WIDTH, mk, mv, batch=batch, seq=seq)
    return _finish_layer(x, o_main, o_mem, lw, shared)


def kernel(x_prompt, x_sample, mem_prompt, state_gla, cache_sb_k, cache_sb_v, cache_mem_k, cache_mem_v,
           w_in_a, w_gate_lr, b_gate, gla_norm_g, w_in_b, w_kv_shared, w_mem_kv, w_o, ln1_g, ln1_b,
           ln2_g, ln2_b, w_ffn_in, w_ffn_out):
    bp, lp, _ = x_prompt.shape
    bs, ls, _ = x_sample.shape
    mlen = mem_prompt.shape[1]

    shared = {
        "w_o": w_o.astype(BF16),
        "w_ffn_in": w_ffn_in.astype(BF16),
        "w_ffn_out": w_ffn_out.astype(BF16),
        "w_in_b": w_in_b.astype(BF16),
        "in_b_scale": _scale_in_b(),
    }
    layers = []
    for l in range(DEPTH):
        lw = {
            "layer": l,
            "ln1_g": ln1_g[l][None, :], "ln1_b": ln1_b[l][None, :],
            "ln2_g": ln2_g[l][None, :], "ln2_b": ln2_b[l][None, :],
        }
        if l < N_A:
            lw["w_in"], lw["in_scale"] = _layout_in_a(w_in_a[l])
            lr2 = _pad_heads(w_gate_lr[l], GLA_DK, GLA_DKP)
            lw["w_lr2p"] = jnp.pad(lr2, ((0, GLA_LRP - GLA_LOWRANK), (0, 0))).astype(BF16)
            lw["b_gatep"] = _pad_heads(b_gate[l][None, :], GLA_DK, GLA_DKP)
            lw["g_norm"] = gla_norm_g[l][None, :]
        layers.append(lw)
    w_kv_b = w_kv_shared.astype(BF16)
    w_memkv_b = w_mem_kv.astype(BF16)

    xp = x_prompt.reshape(bp * lp, D_MODEL)
    xs = x_sample.reshape(bs * ls, D_MODEL)
    xpb, xsb = xp, xs
    memb = mem_prompt.reshape(bp * mlen, D_MODEL).astype(BF16)
    cmk = cache_mem_k.reshape(DEPTH, bs, mlen, MEM_WIDTH).astype(BF16)
    cmv = cache_mem_v.reshape(DEPTH, bs, mlen, MEM_WIDTH).astype(BF16)

    s_zero = jnp.zeros((bp, GLA_HEADS, GLA_DK, GLA_DV), F32)
    gla_p, gla_s, mk_list, mv_list = [], [], [], []
    for l in range(DEPTH):
        lw = layers[l]
        mkv, = _matmul(memb, w_memkv_b, layer=l, out_dtypes=(F32,), tm=1024, tn=1024)
        mk_p, mv_p = mkv[:, :MEM_WIDTH], mkv[:, MEM_WIDTH:]
        mk_list.append(mk_p.reshape(bp, mlen, MEM_HEADS, HEAD_DIM))
        mv_list.append(mv_p.reshape(bp, mlen, MEM_HEADS, HEAD_DIM))
        mk_pb = mk_p.astype(BF16).reshape(bp, mlen, MEM_WIDTH)
        mv_pb = mv_p.astype(BF16).reshape(bp, mlen, MEM_WIDTH)
        if l < N_A:
            xp, xpb, sp = _layer_a(xp, xpb, lw, shared, mk_pb, mv_pb, s_zero, batch=bp, seq=lp)
            xs, xsb, ss = _layer_a(xs, xsb, lw, shared, cmk[l], cmv[l], state_gla[l], batch=bs, seq=ls)
            gla_p.append(sp)
            gla_s.append(ss)
        else:
            if l == N_A:
                kp_f, kp_b = _kv_proj(xpb, w_kv_b, 0, batch=bp, seq=lp)
                vp_f, vp_b = _kv_proj(xpb, w_kv_b, 1, batch=bp, seq=lp)
                ks_f, ks_b = _kv_proj(xsb, w_kv_b, 0, batch=bs, seq=ls)
                vs_f, vs_b = _kv_proj(xsb, w_kv_b, 1, batch=bs, seq=ls)
                lk = PAST_LEN + ls
                lk_pad = -(-lk // SB_TILE) * SB_TILE

                def with_cache(cache, new_b):
                    allk = jnp.concatenate([cache.reshape(bs, PAST_LEN, MAIN_WIDTH).astype(BF16),
                                            new_b.reshape(bs, ls, MAIN_WIDTH)], axis=1)
                    allk = jnp.pad(allk, ((0, 0), (0, lk_pad - lk), (0, 0)))
                    return allk.reshape(bs * lk_pad, MAIN_WIDTH)

                k_all_b = with_cache(cache_sb_k, ks_b)
                v_all_b = with_cache(cache_sb_v, vs_b)
            xp, xpb = _layer_b(xp, xpb, lw, shared, mk_pb, mv_pb, kp_b, vp_b, batch=bp, seq=lp, seq_k=lp,
                               q_off=0)
            xs, xsb = _layer_b(xs, xsb, lw, shared, cmk[l], cmv[l], k_all_b, v_all_b, batch=bs, seq=ls,
                               seq_k=lk_pad, q_off=PAST_LEN)

    return (xp.reshape(bp, lp, D_MODEL), xs.reshape(bs, ls, D_MODEL),
            jnp.stack(gla_p, 0), jnp.stack(gla_s, 0), kp_f, vp_f, ks_f, vs_f,
            jnp.stack(mk_list, 0), jnp.stack(mv_list, 0))
```

```python
import functools
import math

import numpy as np
import jax
import jax.numpy as jnp
from jax import lax
from jax.experimental import pallas as pl
from jax.experimental.pallas import tpu as pltpu

D_MODEL = 2048
DEPTH = 4
PAST_LEN = 2048
CHUNK = 64
N_A = DEPTH // 2
HEAD_DIM = 128
MEM_HEADS = 4
MEM_WIDTH = MEM_HEADS * HEAD_DIM
MAIN_WIDTH = D_MODEL - MEM_WIDTH
GLA_HEADS = 4
GLA_DV = MAIN_WIDTH // GLA_HEADS
GLA_DK = GLA_DV // 2
GLA_KW = GLA_HEADS * GLA_DK
GLA_LOWRANK = 16
GLA_TAU = 16.0
SB_HEADS = MAIN_WIDTH // HEAD_DIM
D_FF = ((8 * D_MODEL // 3 + 255) // 256) * 256
DN_ALPHA = (2 * DEPTH) ** 0.25
LN_EPS = 1e-5
RMS_EPS = 1e-6
SB_SCALE = HEAD_DIM ** -0.5
MEM_SCALE = HEAD_DIM ** -0.5

V7X_LANES = 128
V7X_MXU_DIM = 256
V7X_VMEM_BYTES = 64 * 1024 * 1024

GLA_DKP = V7X_MXU_DIM
GLA_LRP = V7X_LANES
A_OFF_V = 0
A_OFF_G = MAIN_WIDTH
A_OFF_Q = 2 * MAIN_WIDTH
A_OFF_K = A_OFF_Q + GLA_HEADS * GLA_DKP
A_OFF_QM = A_OFF_K + GLA_HEADS * GLA_DKP
A_OFF_LR = A_OFF_QM + MEM_WIDTH
A_WIDTH = A_OFF_LR + GLA_LRP

SB_TILE = 256
SB_HEADS_PER_STEP = 4
SB_DEAD_LOG2 = 150.0
LOG2_E = math.log2(math.e)

F32 = jnp.float32
BF16 = jnp.bfloat16


def _vmem_limit(block_bytes, scratch_bytes=0, temp_bytes=0):
    need = 2 * block_bytes + scratch_bytes + temp_bytes + (2 << 20)
    return int(min(max(need, 16 << 20), V7X_VMEM_BYTES - (8 << 20)))


def _nbytes(shape, dtype):
    return int(np.prod(shape)) * jnp.dtype(dtype).itemsize


def _tile(n, pref):
    if n <= pref:
        return n
    t = pref
    while n % t:
        t -= 1
    return t


def _dot(a, b):
    return jnp.dot(a, b, preferred_element_type=F32)


def _dot_nt(a, b):
    return lax.dot_general(a, b, (((1,), (1,)), ((), ())), preferred_element_type=F32)


def _dot_tn(a, b):
    return lax.dot_general(a, b, (((0,), (0,)), ((), ())), preferred_element_type=F32)


def _sigmoid(x):
    return 1.0 / (1.0 + jnp.exp(-x))


def _softplus(x):
    return jnp.maximum(x, 0.0) + jnp.log(1.0 + jnp.exp(-jnp.abs(x)))


def _mm_kernel(*refs, has_scale):
    if has_scale:
        a_ref, w_ref, s_ref = refs[:3]
        o_refs = refs[3:]
    else:
        a_ref, w_ref = refs[:2]
        o_refs = refs[2:]
    acc = _dot(a_ref[...].astype(BF16), w_ref[...].astype(BF16))
    if has_scale:
        acc = acc * s_ref[...]
    for o_ref in o_refs:
        o_ref[...] = acc.astype(o_ref.dtype)


def _weight_spec(w, block, index_map, layer, **kwargs):
    if w.ndim == 2:
        return pl.BlockSpec(block, index_map, **kwargs)
    return pl.BlockSpec((None,) + block, lambda *g: (layer,) + index_map(*g), **kwargs)


def _matmul(a, w, *, out_dtypes, layer=None, col_scale=None, tm=1024, tn=1024):
    m, k = a.shape
    n = w.shape[-1]
    tm, tn = _tile(m, tm), _tile(n, tn)
    in_specs = [pl.BlockSpec((tm, k), lambda j, i: (i, 0)),
                _weight_spec(w, (k, tn), lambda j, i: (0, j), layer)]
    args = [a, w]
    if col_scale is not None:
        in_specs.append(pl.BlockSpec((1, tn), lambda j, i: (0, j)))
        args.append(col_scale)
    blocks = _nbytes((tm, k), a.dtype) + _nbytes((k, tn), w.dtype) + sum(_nbytes((tm, tn), d) for d in out_dtypes)
    outs = pl.pallas_call(
        functools.partial(_mm_kernel, has_scale=col_scale is not None),
        grid=(n // tn, m // tm),
        in_specs=in_specs,
        out_specs=[pl.BlockSpec((tm, tn), lambda j, i: (i, j)) for _ in out_dtypes],
        out_shape=[jax.ShapeDtypeStruct((m, n), d) for d in out_dtypes],
        compiler_params=pltpu.CompilerParams(
            dimension_semantics=("arbitrary", "arbitrary"),
            vmem_limit_bytes=_vmem_limit(blocks, temp_bytes=2 * _nbytes((tm, tn), F32))),
        name="proj_matmul",
    )(*args)
    return outs


def _swiglu_kernel(a_ref, wg_ref, wu_ref, o_ref, *, nsplit):
    rows_per = a_ref.shape[0] // nsplit
    wg = wg_ref[...].astype(BF16)
    wu = wu_ref[...].astype(BF16)
    for r in range(nsplit):
        rows = slice(r * rows_per, (r + 1) * rows_per)
        a = a_ref[rows, :]
        gate = _dot(a, wg)
        up = _dot(a, wu)
        o_ref[rows, :] = (gate * _sigmoid(gate) * up).astype(o_ref.dtype)


def _ffn_in(a, w, layer, *, tm=2048, tn=512):
    m, k = a.shape
    tm, tn = _tile(m, tm), _tile(D_FF, tn)
    nj = D_FF // tn
    blocks = _nbytes((tm, k), BF16) + 2 * _nbytes((k, tn), w.dtype) + _nbytes((tm, tn), BF16)
    return pl.pallas_call(
        functools.partial(_swiglu_kernel, nsplit=4 if tm % 1024 == 0 else 1),
        grid=(nj, m // tm),
        in_specs=[pl.BlockSpec((tm, k), lambda j, i: (i, 0)),
                  _weight_spec(w, (k, tn), lambda j, i: (0, j), layer),
                  _weight_spec(w, (k, tn), lambda j, i: (0, j + nj), layer)],
        out_specs=pl.BlockSpec((tm, tn), lambda j, i: (i, j)),
        out_shape=jax.ShapeDtypeStruct((m, D_FF), BF16),
        compiler_params=pltpu.CompilerParams(
            dimension_semantics=("arbitrary", "arbitrary"),
            vmem_limit_bytes=_vmem_limit(blocks, temp_bytes=4 * _nbytes((tm, tn), F32))),
        name="ffn_in_swiglu",
    )(a, w, w)


def _ln_kernel(*refs, nparts, nsplit):
    a_refs = refs[:nparts]
    w_ref, x_ref, g_ref, b_ref, of_ref, ob_ref = refs[nparts:]
    rows_per = x_ref.shape[0] // nsplit
    for r in range(nsplit):
        rows = slice(r * rows_per, (r + 1) * rows_per)
        a = a_refs[0][rows, :] if nparts == 1 else jnp.concatenate([p[rows, :] for p in a_refs], axis=1)
        y = DN_ALPHA * x_ref[rows, :] + _dot(a, w_ref[...])
        mu = jnp.mean(y, axis=-1, keepdims=True)
        d = y - mu
        var = jnp.mean(d * d, axis=-1, keepdims=True)
        out = d * lax.rsqrt(var + LN_EPS) * g_ref[...] + b_ref[...]
        of_ref[rows, :] = out
        ob_ref[rows, :] = out.astype(ob_ref.dtype)


def _matmul_residual_ln(a_parts, w, layer, x, gamma, beta, *, tm, nsplit=1):
    m, n = x.shape
    k = w.shape[-2]
    assert sum(a.shape[1] for a in a_parts) == k
    tm = _tile(m, tm)
    assert tm % nsplit == 0
    in_specs = [pl.BlockSpec((tm, a.shape[1]), lambda i: (i, 0)) for a in a_parts]
    in_specs += [_weight_spec(w, (k, n), lambda i: (0, 0), layer, pipeline_mode=pl.Buffered(1)),
                 pl.BlockSpec((tm, n), lambda i: (i, 0)),
                 pl.BlockSpec((1, n), lambda i: (0, 0)),
                 pl.BlockSpec((1, n), lambda i: (0, 0))]
    args = list(a_parts) + [w]
    resident = _nbytes((k, n), BF16)
    blocks = _nbytes((tm, k), BF16) + 2 * _nbytes((tm, n), F32) + _nbytes((tm, n), BF16)
    return pl.pallas_call(
        functools.partial(_ln_kernel, nparts=len(a_parts), nsplit=nsplit),
        grid=(m // tm,),
        in_specs=in_specs,
        out_specs=[pl.BlockSpec((tm, n), lambda i: (i, 0)),
                   pl.BlockSpec((tm, n), lambda i: (i, 0))],
        out_shape=[jax.ShapeDtypeStruct((m, n), F32), jax.ShapeDtypeStruct((m, n), BF16)],
        compiler_params=pltpu.CompilerParams(
            dimension_semantics=("arbitrary",),
            vmem_limit_bytes=_vmem_limit(blocks, scratch_bytes=resident,
                                         temp_bytes=3 * _nbytes((tm, n), F32))),
        name="matmul_residual_ln",
    )(*args, x, gamma, beta)


def _kv_kernel(a_ref, w_ref, of_ref, ob_ref):
    acc = _dot(a_ref[...], w_ref[...].astype(BF16))
    ob_ref[...] = acc.astype(ob_ref.dtype)
    for h in range(SB_HEADS):
        of_ref[0, :, h, :] = acc[:, h * HEAD_DIM:(h + 1) * HEAD_DIM]


def _kv_proj(a, w, col_blk, *, batch, seq, tm=512):
    m, k = a.shape
    tm = _tile(seq, tm)
    nb = seq // tm
    sublane_padded_heads = -(-SB_HEADS // 8) * 8
    blocks = (_nbytes((tm, k), BF16) + _nbytes((tm, MAIN_WIDTH), BF16)
              + _nbytes((tm, sublane_padded_heads, HEAD_DIM), F32))
    return pl.pallas_call(
        _kv_kernel,
        grid=(m // tm,),
        in_specs=[pl.BlockSpec((tm, k), lambda i: (i, 0)),
                  pl.BlockSpec((k, MAIN_WIDTH), lambda i: (0, col_blk), pipeline_mode=pl.Buffered(1))],
        out_specs=[pl.BlockSpec((1, tm, SB_HEADS, HEAD_DIM), lambda i: (i // nb, i % nb, 0, 0)),
                   pl.BlockSpec((tm, MAIN_WIDTH), lambda i: (i, 0))],
        out_shape=[jax.ShapeDtypeStruct((batch, seq, SB_HEADS, HEAD_DIM), F32),
                   jax.ShapeDtypeStruct((m, MAIN_WIDTH), BF16)],
        compiler_params=pltpu.CompilerParams(
            dimension_semantics=("arbitrary",),
            vmem_limit_bytes=_vmem_limit(blocks, scratch_bytes=_nbytes((k, MAIN_WIDTH), w.dtype),
                                         temp_bytes=2 * _nbytes((tm, MAIN_WIDTH), F32))),
        name="kv_proj",
    )(a, w)


def _gla_tables(c):
    nlev = int(math.log2(c))
    assert 1 << nlev == c
    idx = np.arange(c)
    i, t = idx[:, None], idx[None, :]
    sums, pairs = [], []
    for lev in range(nlev):
        h = c >> (lev + 1)
        p = (i // (2 * h)) * 2 * h + h - 1
        second = (i % (2 * h)) >= h
        sums.append(np.where(second, (t > p) & (t <= i), (t > i) & (t <= p)))
        pairs.append((i // (2 * h) == t // (2 * h)) & second & ((t % (2 * h)) < h))
    sums.append(t <= i)
    sums.append(t > i)
    pairs.append(i == t)
    return (np.concatenate(sums, 0).astype(np.float32), np.stack(pairs, 0).astype(np.float32), nlev)


def _gla_kernel(v_ref, g_ref, q_ref, k_ref, lr_ref, wlr_ref, bg_ref, gn_ref, sum_ref, pair_ref, s0_ref,
                o_ref, sf_ref, st_ref, la_ref, og_ref, *, c, nchunk, nlev, nt):
    t = pl.program_id(1)

    @pl.when(t == 0)
    def _():
        pad = jnp.zeros((GLA_DKP - GLA_DK, GLA_DV), F32)
        for h in range(GLA_HEADS):
            st_ref[h] = jnp.concatenate([s0_ref[0, h], pad], axis=0).T

    zg = _dot(lr_ref[...], wlr_ref[...]) + bg_ref[...]
    la_ref[...] = (-_softplus(-zg) * (1.0 / GLA_TAU)).astype(BF16)

    def head(h, rows):
        kc = slice(h * GLA_DKP, (h + 1) * GLA_DKP)
        vc = slice(h * GLA_DV, (h + 1) * GLA_DV)
        w = jnp.exp(_dot(sum_ref[...], la_ref[rows, kc]))
        qb = q_ref[rows, kc]
        kb = k_ref[rows, kc]
        vb = v_ref[rows, vc]
        q = qb.astype(F32)
        k = kb.astype(F32)
        att = pair_ref[nlev] * _dot_nt(qb, kb)
        for lev in range(nlev):
            wl = w[lev * c:(lev + 1) * c]
            att = att + pair_ref[lev] * _dot_nt((q * wl).astype(BF16), (k * wl).astype(BF16))
        w_pre = w[nlev * c:(nlev + 1) * c]
        w_suf = w[(nlev + 1) * c:(nlev + 2) * c]
        st = st_ref[h]
        og_ref[rows, vc] = _dot(att.astype(BF16), vb) + _dot_nt((q * w_pre).astype(BF16), st.astype(BF16))
        st_ref[h] = st * w_pre[c - 1:c, :] + _dot_tn(vb, (k * w_suf).astype(BF16))

    def chunk(ci, carry):
        rows = pl.ds(pl.multiple_of(ci * c, c), c)
        for h in range(GLA_HEADS):
            head(h, rows)
        return carry

    lax.fori_loop(0, nchunk, chunk, 0, unroll=min(2, nchunk))

    for h in range(GLA_HEADS):
        vc = slice(h * GLA_DV, (h + 1) * GLA_DV)
        o = og_ref[:, vc]
        ms = jnp.mean(o * o, axis=-1, keepdims=True)
        on = o * lax.rsqrt(ms + RMS_EPS) * gn_ref[...]
        g = g_ref[:, vc].astype(F32)
        o_ref[:, vc] = (on * (g * _sigmoid(g))).astype(o_ref.dtype)

    @pl.when(t == nt - 1)
    def _():
        for h in range(GLA_HEADS):
            sf_ref[0, h] = st_ref[h].T[:GLA_DK, :]


def _gla(proj, w_lr2p, b_gatep, g_norm, s0, *, batch, seq):
    c = CHUNK if seq % CHUNK == 0 else seq
    tt = _tile(seq, 512)
    if tt % c:
        tt = c
    nt = seq // tt
    sums_np, pairs_np, nlev = _gla_tables(c)
    sums = jnp.asarray(sums_np, BF16)
    pairs = jnp.asarray(pairs_np, F32)
    nrow = sums_np.shape[0]
    kw = GLA_HEADS * GLA_DKP
    state_blk = (1, GLA_HEADS, GLA_DK, GLA_DV)
    state_vmem = (GLA_HEADS, GLA_DV, GLA_DKP)

    in_specs = [
        pl.BlockSpec((tt, MAIN_WIDTH), lambda b, t: (b * nt + t, A_OFF_V // MAIN_WIDTH)),
        pl.BlockSpec((tt, MAIN_WIDTH), lambda b, t: (b * nt + t, A_OFF_G // MAIN_WIDTH)),
        pl.BlockSpec((tt, kw), lambda b, t: (b * nt + t, A_OFF_Q // kw)),
        pl.BlockSpec((tt, kw), lambda b, t: (b * nt + t, A_OFF_K // kw)),
        pl.BlockSpec((tt, GLA_LRP), lambda b, t: (b * nt + t, A_OFF_LR // GLA_LRP)),
        pl.BlockSpec((GLA_LRP, kw), lambda b, t: (0, 0)),
        pl.BlockSpec((1, kw), lambda b, t: (0, 0)),
        pl.BlockSpec((1, GLA_DV), lambda b, t: (0, 0)),
        pl.BlockSpec((nrow, c), lambda b, t: (0, 0)),
        pl.BlockSpec((nlev + 1, c, c), lambda b, t: (0, 0, 0)),
        pl.BlockSpec(state_blk, lambda b, t: (b, 0, 0, 0)),
    ]
    blocks = (3 * _nbytes((tt, MAIN_WIDTH), BF16) + 2 * _nbytes((tt, kw), BF16) + _nbytes((tt, GLA_LRP), BF16)
              + _nbytes((GLA_LRP, kw), BF16) + 2 * _nbytes(state_blk, F32) + _nbytes((nrow, c), BF16)
              + _nbytes((nlev + 1, c, c), F32))
    return pl.pallas_call(
        functools.partial(_gla_kernel, c=c, nchunk=tt // c, nlev=nlev, nt=nt),
        grid=(batch, nt),
        in_specs=in_specs,
        out_specs=[pl.BlockSpec((tt, MAIN_WIDTH), lambda b, t: (b * nt + t, 0)),
                   pl.BlockSpec(state_blk, lambda b, t: (b, 0, 0, 0))],
        out_shape=[jax.ShapeDtypeStruct((batch * seq, MAIN_WIDTH), BF16),
                   jax.ShapeDtypeStruct((batch, GLA_HEADS, GLA_DK, GLA_DV), F32)],
        scratch_shapes=[pltpu.VMEM(state_vmem, F32), pltpu.VMEM((tt, kw), BF16),
                        pltpu.VMEM((tt, MAIN_WIDTH), F32)],
        compiler_params=pltpu.CompilerParams(
            dimension_semantics=("arbitrary", "arbitrary"),
            vmem_limit_bytes=_vmem_limit(
                blocks, temp_bytes=8 << 20,
                scratch_bytes=_nbytes(state_vmem, F32) + _nbytes((tt, kw), BF16) + _nbytes((tt, MAIN_WIDTH), F32))),
        name="gla_scan",
    )(proj, proj, proj, proj, proj, w_lr2p, b_gatep, g_norm, sums, pairs, s0)


def _sb_kernel(q_ref, k_ref, v_ref, u_ref, o_ref, acc_ref, carry_ref, *, tq, tk, q_off, nh):
    qi = pl.program_id(2)
    jd = (q_off + qi * tq) // tk
    q_pos = q_off + qi * tq + lax.broadcasted_iota(jnp.int32, (tq, tk), 0)
    ntile = tk // V7X_LANES

    def tile(h, j, carry, reach):
        cols = slice(h * HEAD_DIM, (h + 1) * HEAD_DIM)
        keys = pl.ds(pl.multiple_of(j * tk, tk), tk)
        z = _dot_nt(q_ref[:, cols], k_ref[keys, cols])
        sp = jnp.maximum(z, 0.0) + jnp.log2(1.0 + jnp.exp2(-jnp.abs(z)))
        if reach is not None:
            sp = jnp.where(reach, sp, 0.0)
        cs = _dot(sp.astype(BF16), u_ref[...])
        w = jnp.exp2(z - cs - jnp.concatenate([carry] * ntile, axis=1))
        if reach is not None:
            w = jnp.where(reach, w, 0.0)
        return _dot(w.astype(BF16), v_ref[keys, cols]), jnp.sum(sp, axis=1, keepdims=True)

    def live(carries):
        low = functools.reduce(jnp.minimum, carries)
        return jnp.min(low, axis=0, keepdims=True)[0, 0] < SB_DEAD_LOG2

    k_col = lax.broadcasted_iota(jnp.int32, (tq, tk), 1)
    diag_reach = (jd * tk + k_col) < q_pos
    has_prev = jnp.where(jd >= 1, 1.0, 0.0)
    carries = []
    for h in range(nh):
        a0, r0 = tile(h, jd, jnp.zeros((tq, V7X_LANES), F32), diag_reach)
        c0 = jnp.broadcast_to(r0, (tq, V7X_LANES))
        a1, r1 = tile(h, jnp.maximum(jd - 1, 0), c0, None)
        c1 = c0 + r1 * has_prev
        acc_ref[h] = a0 + a1 * has_prev
        carry_ref[h] = c1
        carries.append(c1)

    def cond(state):
        j, alive = state
        return jnp.logical_and(j >= 0, alive)

    def body(state):
        j, _ = state
        new = []
        for h in range(nh):
            carry = carry_ref[h]
            a, r = tile(h, j, carry, None)
            acc_ref[h] += a
            carry_ref[h] = carry + r
            new.append(carry + r)
        return j - 1, live(new)

    lax.while_loop(cond, body, (jd - 2, live(carries)))
    for h in range(nh):
        o_ref[:, h * HEAD_DIM:(h + 1) * HEAD_DIM] = acc_ref[h].astype(o_ref.dtype)


def _sb_attention(q_arr, q_col0, k_arr, v_arr, *, batch, seq_q, seq_k, q_off):
    tq = min(SB_TILE, seq_q)
    tk = SB_TILE
    nh = SB_HEADS_PER_STEP
    width = nh * HEAD_DIM
    assert seq_q % tq == 0 and seq_k % tk == 0 and q_off % tk == 0 and tq <= tk
    assert q_col0 % width == 0 and SB_HEADS % nh == 0
    nq = seq_q // tq
    u = jnp.asarray(np.tril(np.ones((tk, tk), np.float32)), BF16)
    blocks = 2 * _nbytes((tq, width), BF16) + 2 * _nbytes((seq_k, width), BF16) + _nbytes((tk, tk), BF16)
    return pl.pallas_call(
        functools.partial(_sb_kernel, tq=tq, tk=tk, q_off=q_off, nh=nh),
        grid=(batch, SB_HEADS // nh, nq),
        in_specs=[pl.BlockSpec((tq, width), lambda b, h, i: (b * nq + i, q_col0 // width + h)),
                  pl.BlockSpec((seq_k, width), lambda b, h, i: (b, h)),
                  pl.BlockSpec((seq_k, width), lambda b, h, i: (b, h)),
                  pl.BlockSpec((tk, tk), lambda b, h, i: (0, 0))],
        out_specs=pl.BlockSpec((tq, width), lambda b, h, i: (b * nq + i, h)),
        out_shape=jax.ShapeDtypeStruct((batch * seq_q, MAIN_WIDTH), BF16),
        scratch_shapes=[pltpu.VMEM((nh, tq, HEAD_DIM), F32), pltpu.VMEM((nh, tq, V7X_LANES), F32)],
        compiler_params=pltpu.CompilerParams(
            dimension_semantics=("arbitrary", "arbitrary", "arbitrary"),
            vmem_limit_bytes=_vmem_limit(blocks, temp_bytes=12 * nh * _nbytes((tq, tk), F32))),
        name="stick_breaking",
    )(q_arr, k_arr, v_arr, u)


def _mem_kernel(qm_ref, mk_ref, mv_ref, o_ref):
    for h in range(MEM_HEADS):
        cols = slice(h * HEAD_DIM, (h + 1) * HEAD_DIM)
        s = _dot_nt(qm_ref[:, cols], mk_ref[0, :, cols])
        e = jnp.exp(s - jnp.max(s, axis=-1, keepdims=True))
        den = jnp.sum(e, axis=-1, keepdims=True)
        o = _dot(e.astype(BF16), mv_ref[0, :, cols]) / den
        o_ref[:, cols] = o.astype(o_ref.dtype)


def _mem_attention(q_arr, q_col0, mk, mv, *, batch, seq):
    tt = _tile(seq, 512)
    nt = seq // tt
    mlen = mk.shape[1]
    assert q_col0 % MEM_WIDTH == 0
    blocks = 2 * _nbytes((tt, MEM_WIDTH), BF16) + 2 * _nbytes((mlen, MEM_WIDTH), BF16)
    return pl.pallas_call(
        _mem_kernel,
        grid=(batch, nt),
        in_specs=[pl.BlockSpec((tt, MEM_WIDTH), lambda b, t: (b * nt + t, q_col0 // MEM_WIDTH)),
                  pl.BlockSpec((1, mlen, MEM_WIDTH), lambda b, t: (b, 0, 0)),
                  pl.BlockSpec((1, mlen, MEM_WIDTH), lambda b, t: (b, 0, 0))],
        out_specs=pl.BlockSpec((tt, MEM_WIDTH), lambda b, t: (b * nt + t, 0)),
        out_shape=jax.ShapeDtypeStruct((batch * seq, MEM_WIDTH), BF16),
        compiler_params=pltpu.CompilerParams(
            dimension_semantics=("arbitrary", "arbitrary"),
            vmem_limit_bytes=_vmem_limit(blocks, temp_bytes=6 * _nbytes((tt, mlen), F32))),
        name="mem_attention",
    )(q_arr, mk, mv)


def _pad_heads(w, width, padded):
    lead = w.shape[:-1]
    w = w.reshape(lead + (GLA_HEADS, width))
    w = jnp.pad(w, [(0, 0)] * len(lead) + [(0, 0), (0, padded - width)])
    return w.reshape(lead + (GLA_HEADS * padded,))


def _layout_in_a(w_in):
    q = w_in[:, :GLA_KW]
    k = w_in[:, GLA_KW:2 * GLA_KW]
    v = w_in[:, 2 * GLA_KW:2 * GLA_KW + MAIN_WIDTH]
    g = w_in[:, 2 * GLA_KW + MAIN_WIDTH:2 * GLA_KW + 2 * MAIN_WIDTH]
    lr = w_in[:, 2 * GLA_KW + 2 * MAIN_WIDTH:2 * GLA_KW + 2 * MAIN_WIDTH + GLA_LOWRANK]
    qm = w_in[:, 2 * GLA_KW + 2 * MAIN_WIDTH + GLA_LOWRANK:]
    lr = jnp.pad(lr, ((0, 0), (0, GLA_LRP - GLA_LOWRANK)))
    w = jnp.concatenate([v, g, _pad_heads(q, GLA_DK, GLA_DKP), _pad_heads(k, GLA_DK, GLA_DKP), qm, lr], axis=1)
    scale = np.ones((1, A_WIDTH), np.float32)
    scale[:, A_OFF_Q:A_OFF_K] = GLA_DK ** -0.5
    scale[:, A_OFF_QM:A_OFF_LR] = MEM_SCALE
    return w.astype(BF16), jnp.asarray(scale)


def _scale_in_b():
    scale = np.ones((1, D_MODEL), np.float32)
    scale[:, :MAIN_WIDTH] = SB_SCALE * LOG2_E
    scale[:, MAIN_WIDTH:] = MEM_SCALE
    return jnp.asarray(scale)


def _finish_layer(x, o_main, o_mem, lw, shared):
    l = lw["layer"]
    x1, x1b = _matmul_residual_ln([o_main, o_mem], shared["w_o"], l, x, lw["ln1_g"], lw["ln1_b"], tm=512,
                                  nsplit=4)
    hid = _ffn_in(x1b, shared["w_ffn_in"], l)
    return _matmul_residual_ln([hid], shared["w_ffn_out"], l, x1, lw["ln2_g"], lw["ln2_b"], tm=256)


def _layer_a(x, xb, lw, shared, mk, mv, s0, *, batch, seq):
    proj, = _matmul(xb, lw["w_in"], out_dtypes=(BF16,), col_scale=lw["in_scale"], tm=1024, tn=1920)
    o_main, s_final = _gla(proj, lw["w_lr2p"], lw["b_gatep"], lw["g_norm"], s0, batch=batch, seq=seq)
    o_mem = _mem_attention(proj, A_OFF_QM, mk, mv, batch=batch, seq=seq)
    x, xb = _finish_layer(x, o_main, o_mem, lw, shared)
    return x, xb, s_final


def _layer_b(x, xb, lw, shared, mk, mv, kb, vb, *, batch, seq, seq_k, q_off):
    proj, = _matmul(xb, shared["w_in_b"], layer=lw["layer"] - N_A, out_dtypes=(BF16,),
                    col_scale=shared["in_b_scale"], tm=1024, tn=1024)
    o_main = _sb_attention(proj, 0, kb, vb, batch=batch, seq_q=seq, seq_k=seq_k, q_off=q_off)
    o_mem = _mem_attention(proj, MAIN_WIDTH, mk, mv, batch=batch, seq=seq)
    return _finish_layer(x, o_main, o_mem, lw, shared)


def kernel(x_prompt, x_sample, mem_prompt, state_gla, cache_sb_k, cache_sb_v, cache_mem_k, cache_mem_v,
           w_in_a, w_gate_lr, b_gate, gla_norm_g, w_in_b, w_kv_shared, w_mem_kv, w_o, ln1_g, ln1_b,
           ln2_g, ln2_b, w_ffn_in, w_ffn_out):
    bp, lp, _ = x_prompt.shape
    bs, ls, _ = x_sample.shape
    mlen = mem_prompt.shape[1]

    shared = {
        "w_o": w_o.astype(BF16),
        "w_ffn_in": w_ffn_in,
        "w_ffn_out": w_ffn_out.astype(BF16),
        "w_in_b": w_in_b,
        "in_b_scale": _scale_in_b(),
    }
    layers = []
    for l in range(DEPTH):
        lw = {
            "layer": l,
            "ln1_g": ln1_g[l][None, :], "ln1_b": ln1_b[l][None, :],
            "ln2_g": ln2_g[l][None, :], "ln2_b": ln2_b[l][None, :],
        }
        if l < N_A:
            lw["w_in"], lw["in_scale"] = _layout_in_a(w_in_a[l])
            lr2 = _pad_heads(w_gate_lr[l], GLA_DK, GLA_DKP)
            lw["w_lr2p"] = jnp.pad(lr2, ((0, GLA_LRP - GLA_LOWRANK), (0, 0))).astype(BF16)
            lw["b_gatep"] = _pad_heads(b_gate[l][None, :], GLA_DK, GLA_DKP)
            lw["g_norm"] = gla_norm_g[l][None, :]
        layers.append(lw)
    w_kv = w_kv_shared
    w_memkv = w_mem_kv

    xp = x_prompt.reshape(bp * lp, D_MODEL)
    xs = x_sample.reshape(bs * ls, D_MODEL)
    xpb, xsb = xp, xs
    memb = mem_prompt.reshape(bp * mlen, D_MODEL).astype(BF16)
    cmk = cache_mem_k.reshape(DEPTH, bs, mlen, MEM_WIDTH).astype(BF16)
    cmv = cache_mem_v.reshape(DEPTH, bs, mlen, MEM_WIDTH).astype(BF16)

    s_zero = jnp.zeros((bp, GLA_HEADS, GLA_DK, GLA_DV), F32)
    gla_p, gla_s, mk_list, mv_list = [], [], [], []
    for l in range(DEPTH):
        lw = layers[l]
        mkv, = _matmul(memb, w_memkv, layer=l, out_dtypes=(F32,), tm=1024, tn=1024)
        mk_p, mv_p = mkv[:, :MEM_WIDTH], mkv[:, MEM_WIDTH:]
        mk_list.append(mk_p.reshape(bp, mlen, MEM_HEADS, HEAD_DIM))
        mv_list.append(mv_p.reshape(bp, mlen, MEM_HEADS, HEAD_DIM))
        mk_pb = mk_p.astype(BF16).reshape(bp, mlen, MEM_WIDTH)
        mv_pb = mv_p.astype(BF16).reshape(bp, mlen, MEM_WIDTH)
        if l < N_A:
            xp, xpb, sp = _layer_a(xp, xpb, lw, shared, mk_pb, mv_pb, s_zero, batch=bp, seq=lp)
            xs, xsb, ss = _layer_a(xs, xsb, lw, shared, cmk[l], cmv[l], state_gla[l], batch=bs, seq=ls)
            gla_p.append(sp)
            gla_s.append(ss)
        else:
            if l == N_A:
                kp_f, kp_b = _kv_proj(xpb, w_kv, 0, batch=bp, seq=lp)
                vp_f, vp_b = _kv_proj(xpb, w_kv, 1, batch=bp, seq=lp)
                ks_f, ks_b = _kv_proj(xsb, w_kv, 0, batch=bs, seq=ls)
                vs_f, vs_b = _kv_proj(xsb, w_kv, 1, batch=bs, seq=ls)
                lk = PAST_LEN + ls
                lk_pad = -(-lk // SB_TILE) * SB_TILE

                def with_cache(cache, new_b):
                    allk = jnp.concatenate([cache.reshape(bs, PAST_LEN, MAIN_WIDTH).astype(BF16),
                                            new_b.reshape(bs, ls, MAIN_WIDTH)], axis=1)
                    allk = jnp.pad(allk, ((0, 0), (0, lk_pad - lk), (0, 0)))
                    return allk.reshape(bs * lk_pad, MAIN_WIDTH)

                k_all_b = with_cache(cache_sb_k, ks_b)
                v_all_b = with_cache(cache_sb_v, vs_b)
            xp, xpb = _layer_b(xp, xpb, lw, shared, mk_pb, mv_pb, kp_b, vp_b, batch=bp, seq=lp, seq_k=lp,
                               q_off=0)
            xs, xsb = _layer_b(xs, xsb, lw, shared, cmk[l], cmv[l], k_all_b, v_all_b, batch=bs, seq=ls,
                               seq_k=lk_pad, q_off=PAST_LEN)

    return (xp.reshape(bp, lp, D_MODEL), xs.reshape(bs, ls, D_MODEL),
            jnp.stack(gla_p, 0), jnp.stack(gla_s, 0), kp_f, vp_f, ks_f, vs_f,
            jnp.stack(mk_list, 0), jnp.stack(mv_list, 0))
```

```python
import functools
import math

import numpy as np
import jax
import jax.numpy as jnp
from jax import lax
from jax.experimental import pallas as pl
from jax.experimental.pallas import tpu as pltpu

D_MODEL = 2048
DEPTH = 4
PAST_LEN = 2048
CHUNK = 64
N_A = DEPTH // 2
HEAD_DIM = 128
MEM_HEADS = 4
MEM_WIDTH = MEM_HEADS * HEAD_DIM
MAIN_WIDTH = D_MODEL - MEM_WIDTH
GLA_HEADS = 4
GLA_DV = MAIN_WIDTH // GLA_HEADS
GLA_DK = GLA_DV // 2
GLA_KW = GLA_HEADS * GLA_DK
GLA_LOWRANK = 16
GLA_TAU = 16.0
SB_HEADS = MAIN_WIDTH // HEAD_DIM
D_FF = ((8 * D_MODEL // 3 + 255) // 256) * 256
DN_ALPHA = (2 * DEPTH) ** 0.25
LN_EPS = 1e-5
RMS_EPS = 1e-6
SB_SCALE = HEAD_DIM ** -0.5
MEM_SCALE = HEAD_DIM ** -0.5

V7X_LANES = 128
V7X_MXU_DIM = 256
V7X_VMEM_BYTES = 64 * 1024 * 1024

GLA_DKP = V7X_MXU_DIM
GLA_LRP = V7X_LANES
A_OFF_V = 0
A_OFF_G = MAIN_WIDTH
A_OFF_Q = 2 * MAIN_WIDTH
A_OFF_K = A_OFF_Q + GLA_HEADS * GLA_DKP
A_OFF_QM = A_OFF_K + GLA_HEADS * GLA_DKP
A_OFF_LR = A_OFF_QM + MEM_WIDTH
A_WIDTH = A_OFF_LR + GLA_LRP

SB_TILE = 256
SB_HEADS_PER_STEP = 4
SB_DEAD_LOG2 = 150.0
LOG2_E = math.log2(math.e)

F32 = jnp.float32
BF16 = jnp.bfloat16


def _vmem_limit(block_bytes, scratch_bytes=0, temp_bytes=0):
    need = 2 * block_bytes + scratch_bytes + temp_bytes + (2 << 20)
    return int(min(max(need, 16 << 20), V7X_VMEM_BYTES - (8 << 20)))


def _nbytes(shape, dtype):
    return int(np.prod(shape)) * jnp.dtype(dtype).itemsize


def _tile(n, pref):
    if n <= pref:
        return n
    t = pref
    while n % t:
        t -= 1
    return t


def _dot(a, b):
    return jnp.dot(a, b, preferred_element_type=F32)


def _dot_nt(a, b):
    return lax.dot_general(a, b, (((1,), (1,)), ((), ())), preferred_element_type=F32)


def _dot_tn(a, b):
    return lax.dot_general(a, b, (((0,), (0,)), ((), ())), preferred_element_type=F32)


def _sigmoid(x):
    return 1.0 / (1.0 + jnp.exp(-x))


def _softplus(x):
    return jnp.maximum(x, 0.0) + jnp.log(1.0 + jnp.exp(-jnp.abs(x)))


def _mm_kernel(*refs, has_scale):
    if has_scale:
        a_ref, w_ref, s_ref = refs[:3]
        o_refs = refs[3:]
    else:
        a_ref, w_ref = refs[:2]
        o_refs = refs[2:]
    acc = _dot(a_ref[...].astype(BF16), w_ref[...].astype(BF16))
    if has_scale:
        acc = acc * s_ref[...]
    for o_ref in o_refs:
        o_ref[...] = acc.astype(o_ref.dtype)


def _weight_spec(w, block, index_map, layer, **kwargs):
    if w.ndim == 2:
        return pl.BlockSpec(block, index_map, **kwargs)
    return pl.BlockSpec((None,) + block, lambda *g: (layer,) + index_map(*g), **kwargs)


def _matmul(a, w, *, out_dtypes, layer=None, col_scale=None, tm=1024, tn=1024):
    m, k = a.shape
    n = w.shape[-1]
    tm, tn = _tile(m, tm), _tile(n, tn)
    in_specs = [pl.BlockSpec((tm, k), lambda j, i: (i, 0)),
                _weight_spec(w, (k, tn), lambda j, i: (0, j), layer)]
    args = [a, w]
    if col_scale is not None:
        in_specs.append(pl.BlockSpec((1, tn), lambda j, i: (0, j)))
        args.append(col_scale)
    blocks = _nbytes((tm, k), a.dtype) + _nbytes((k, tn), w.dtype) + sum(_nbytes((tm, tn), d) for d in out_dtypes)
    outs = pl.pallas_call(
        functools.partial(_mm_kernel, has_scale=col_scale is not None),
        grid=(n // tn, m // tm),
        in_specs=in_specs,
        out_specs=[pl.BlockSpec((tm, tn), lambda j, i: (i, j)) for _ in out_dtypes],
        out_shape=[jax.ShapeDtypeStruct((m, n), d) for d in out_dtypes],
        compiler_params=pltpu.CompilerParams(
            dimension_semantics=("arbitrary", "arbitrary"),
            vmem_limit_bytes=_vmem_limit(blocks, temp_bytes=2 * _nbytes((tm, tn), F32))),
        name="proj_matmul",
    )(*args)
    return outs


def _swiglu_kernel(a_ref, wg_ref, wu_ref, o_ref, *, nsplit):
    rows_per = a_ref.shape[0] // nsplit
    wg = wg_ref[...].astype(BF16)
    wu = wu_ref[...].astype(BF16)
    for r in range(nsplit):
        rows = slice(r * rows_per, (r + 1) * rows_per)
        a = a_ref[rows, :]
        gate = _dot(a, wg)
        up = _dot(a, wu)
        o_ref[rows, :] = (gate * _sigmoid(gate) * up).astype(o_ref.dtype)


def _ffn_in(a, w, layer, *, tm=2048, tn=512):
    m, k = a.shape
    tm, tn = _tile(m, tm), _tile(D_FF, tn)
    nj = D_FF // tn
    blocks = _nbytes((tm, k), BF16) + 2 * _nbytes((k, tn), w.dtype) + _nbytes((tm, tn), BF16)
    return pl.pallas_call(
        functools.partial(_swiglu_kernel, nsplit=4 if tm % 1024 == 0 else 1),
        grid=(nj, m // tm),
        in_specs=[pl.BlockSpec((tm, k), lambda j, i: (i, 0)),
                  _weight_spec(w, (k, tn), lambda j, i: (0, j), layer),
                  _weight_spec(w, (k, tn), lambda j, i: (0, j + nj), layer)],
        out_specs=pl.BlockSpec((tm, tn), lambda j, i: (i, j)),
        out_shape=jax.ShapeDtypeStruct((m, D_FF), BF16),
        compiler_params=pltpu.CompilerParams(
            dimension_semantics=("arbitrary", "arbitrary"),
            vmem_limit_bytes=_vmem_limit(blocks, temp_bytes=4 * _nbytes((tm, tn), F32))),
        name="ffn_in_swiglu",
    )(a, w, w)


def _ln_kernel(*refs, nparts, nsplit):
    a_refs = refs[:nparts]
    w_ref, x_ref, g_ref, b_ref, of_ref, ob_ref = refs[nparts:]
    rows_per = x_ref.shape[0] // nsplit
    for r in range(nsplit):
        rows = slice(r * rows_per, (r + 1) * rows_per)
        a = a_refs[0][rows, :] if nparts == 1 else jnp.concatenate([p[rows, :] for p in a_refs], axis=1)
        y = DN_ALPHA * x_ref[rows, :] + _dot(a, w_ref[...])
        mu = jnp.mean(y, axis=-1, keepdims=True)
        d = y - mu
        var = jnp.mean(d * d, axis=-1, keepdims=True)
        out = d * lax.rsqrt(var + LN_EPS) * g_ref[...] + b_ref[...]
        of_ref[rows, :] = out
        ob_ref[rows, :] = out.astype(ob_ref.dtype)


def _matmul_residual_ln(a_parts, w, layer, x, gamma, beta, *, tm, nsplit=1):
    m, n = x.shape
    k = w.shape[-2]
    assert sum(a.shape[1] for a in a_parts) == k
    tm = _tile(m, tm)
    assert tm % nsplit == 0
    in_specs = [pl.BlockSpec((tm, a.shape[1]), lambda i: (i, 0)) for a in a_parts]
    in_specs += [_weight_spec(w, (k, n), lambda i: (0, 0), layer, pipeline_mode=pl.Buffered(1)),
                 pl.BlockSpec((tm, n), lambda i: (i, 0)),
                 pl.BlockSpec((1, n), lambda i: (0, 0)),
                 pl.BlockSpec((1, n), lambda i: (0, 0))]
    args = list(a_parts) + [w]
    resident = _nbytes((k, n), BF16)
    blocks = _nbytes((tm, k), BF16) + 2 * _nbytes((tm, n), F32) + _nbytes((tm, n), BF16)
    return pl.pallas_call(
        functools.partial(_ln_kernel, nparts=len(a_parts), nsplit=nsplit),
        grid=(m // tm,),
        in_specs=in_specs,
        out_specs=[pl.BlockSpec((tm, n), lambda i: (i, 0)),
                   pl.BlockSpec((tm, n), lambda i: (i, 0))],
        out_shape=[jax.ShapeDtypeStruct((m, n), F32), jax.ShapeDtypeStruct((m, n), BF16)],
        compiler_params=pltpu.CompilerParams(
            dimension_semantics=("arbitrary",),
            vmem_limit_bytes=_vmem_limit(blocks, scratch_bytes=resident,
                                         temp_bytes=3 * _nbytes((tm, n), F32))),
        name="matmul_residual_ln",
    )(*args, x, gamma, beta)


def _kv_kernel(a_ref, w_ref, of_ref, ob_ref):
    acc = _dot(a_ref[...], w_ref[...].astype(BF16))
    ob_ref[...] = acc.astype(ob_ref.dtype)
    for h in range(SB_HEADS):
        of_ref[0, :, h, :] = acc[:, h * HEAD_DIM:(h + 1) * HEAD_DIM]


def _kv_dma_kernel(a_ref, w_ref, of_hbm, ob_ref, stage_ref, sem_ref, *, nb, nsteps):
    i = pl.program_id(0)
    tm = a_ref.shape[0]
    slot = i % 2

    def head_copy(step, sl, h):
        rows = pl.ds((step % nb) * tm, tm)
        return pltpu.make_async_copy(stage_ref.at[sl, :, pl.ds(h * HEAD_DIM, HEAD_DIM)],
                                     of_hbm.at[step // nb, rows, h, :], sem_ref.at[sl])

    def wait_step(step, sl):
        for h in range(SB_HEADS):
            head_copy(step, sl, h).wait()

    @pl.when(i >= 2)
    def _():
        wait_step(i - 2, slot)

    acc = _dot(a_ref[...], w_ref[...].astype(BF16))
    ob_ref[...] = acc.astype(ob_ref.dtype)
    stage_ref[slot] = acc
    for h in range(SB_HEADS):
        head_copy(i, slot, h).start()

    @pl.when(i == nsteps - 1)
    def _():
        if nsteps >= 2:
            wait_step(i - 1, 1 - slot)
        wait_step(i, slot)


def _kv_proj_dma(a, w, col_blk, *, batch, seq, tm=512):
    m, k = a.shape
    tm = _tile(seq, tm)
    nb = seq // tm
    nsteps = m // tm
    blocks = _nbytes((tm, k), BF16) + _nbytes((tm, MAIN_WIDTH), BF16)
    return pl.pallas_call(
        functools.partial(_kv_dma_kernel, nb=nb, nsteps=nsteps),
        grid=(nsteps,),
        in_specs=[pl.BlockSpec((tm, k), lambda i: (i, 0)),
                  pl.BlockSpec((k, MAIN_WIDTH), lambda i: (0, col_blk), pipeline_mode=pl.Buffered(1))],
        out_specs=[pl.BlockSpec(memory_space=pl.ANY),
                   pl.BlockSpec((tm, MAIN_WIDTH), lambda i: (i, 0))],
        out_shape=[jax.ShapeDtypeStruct((batch, seq, SB_HEADS, HEAD_DIM), F32),
                   jax.ShapeDtypeStruct((m, MAIN_WIDTH), BF16)],
        scratch_shapes=[pltpu.VMEM((2, tm, MAIN_WIDTH), F32), pltpu.SemaphoreType.DMA((2,))],
        compiler_params=pltpu.CompilerParams(
            dimension_semantics=("arbitrary",),
            vmem_limit_bytes=_vmem_limit(
                blocks, scratch_bytes=_nbytes((k, MAIN_WIDTH), w.dtype) + _nbytes((2, tm, MAIN_WIDTH), F32),
                temp_bytes=2 * _nbytes((tm, MAIN_WIDTH), F32))),
        name="kv_proj_dma",
    )(a, w)


def _kv_proj(a, w, col_blk, *, batch, seq, tm=512):
    m, k = a.shape
    tm = _tile(seq, tm)
    nb = seq // tm
    sublane_padded_heads = -(-SB_HEADS // 8) * 8
    blocks = (_nbytes((tm, k), BF16) + _nbytes((tm, MAIN_WIDTH), BF16)
              + _nbytes((tm, sublane_padded_heads, HEAD_DIM), F32))
    return pl.pallas_call(
        _kv_kernel,
        grid=(m // tm,),
        in_specs=[pl.BlockSpec((tm, k), lambda i: (i, 0)),
                  pl.BlockSpec((k, MAIN_WIDTH), lambda i: (0, col_blk), pipeline_mode=pl.Buffered(1))],
        out_specs=[pl.BlockSpec((1, tm, SB_HEADS, HEAD_DIM), lambda i: (i // nb, i % nb, 0, 0)),
                   pl.BlockSpec((tm, MAIN_WIDTH), lambda i: (i, 0))],
        out_shape=[jax.ShapeDtypeStruct((batch, seq, SB_HEADS, HEAD_DIM), F32),
                   jax.ShapeDtypeStruct((m, MAIN_WIDTH), BF16)],
        compiler_params=pltpu.CompilerParams(
            dimension_semantics=("arbitrary",),
            vmem_limit_bytes=_vmem_limit(blocks, scratch_bytes=_nbytes((k, MAIN_WIDTH), w.dtype),
                                         temp_bytes=2 * _nbytes((tm, MAIN_WIDTH), F32))),
        name="kv_proj",
    )(a, w)


def _gla_tables(c):
    nlev = int(math.log2(c))
    assert 1 << nlev == c
    idx = np.arange(c)
    i, t = idx[:, None], idx[None, :]
    sums, pairs = [], []
    for lev in range(nlev):
        h = c >> (lev + 1)
        p = (i // (2 * h)) * 2 * h + h - 1
        second = (i % (2 * h)) >= h
        sums.append(np.where(second, (t > p) & (t <= i), (t > i) & (t <= p)))
        pairs.append((i // (2 * h) == t // (2 * h)) & second & ((t % (2 * h)) < h))
    sums.append(t <= i)
    sums.append(t > i)
    pairs.append(i == t)
    return (np.concatenate(sums, 0).astype(np.float32), np.stack(pairs, 0).astype(np.float32), nlev)


def _gla_kernel(v_ref, g_ref, q_ref, k_ref, lr_ref, wlr_ref, bg_ref, gn_ref, sum_ref, pair_ref, s0_ref,
                o_ref, sf_ref, st_ref, la_ref, og_ref, *, c, nchunk, nlev, nt):
    t = pl.program_id(1)

    @pl.when(t == 0)
    def _():
        pad = jnp.zeros((GLA_DKP - GLA_DK, GLA_DV), F32)
        for h in range(GLA_HEADS):
            st_ref[h] = jnp.concatenate([s0_ref[0, h], pad], axis=0).T

    zg = _dot(lr_ref[...], wlr_ref[...]) + bg_ref[...]
    la_ref[...] = (-_softplus(-zg) * (1.0 / GLA_TAU)).astype(BF16)

    def head(h, rows):
        kc = slice(h * GLA_DKP, (h + 1) * GLA_DKP)
        vc = slice(h * GLA_DV, (h + 1) * GLA_DV)
        w = jnp.exp(_dot(sum_ref[...], la_ref[rows, kc]))
        qb = q_ref[rows, kc]
        kb = k_ref[rows, kc]
        vb = v_ref[rows, vc]
        q = qb.astype(F32)
        k = kb.astype(F32)
        att = pair_ref[nlev] * _dot_nt(qb, kb)
        for lev in range(nlev):
            wl = w[lev * c:(lev + 1) * c]
            att = att + pair_ref[lev] * _dot_nt((q * wl).astype(BF16), (k * wl).astype(BF16))
        w_pre = w[nlev * c:(nlev + 1) * c]
        w_suf = w[(nlev + 1) * c:(nlev + 2) * c]
        st = st_ref[h]
        og_ref[rows, vc] = _dot(att.astype(BF16), vb) + _dot_nt((q * w_pre).astype(BF16), st.astype(BF16))
        st_ref[h] = st * w_pre[c - 1:c, :] + _dot_tn(vb, (k * w_suf).astype(BF16))

    def chunk(ci, carry):
        rows = pl.ds(pl.multiple_of(ci * c, c), c)
        for h in range(GLA_HEADS):
            head(h, rows)
        return carry

    lax.fori_loop(0, nchunk, chunk, 0, unroll=min(2, nchunk))

    for h in range(GLA_HEADS):
        vc = slice(h * GLA_DV, (h + 1) * GLA_DV)
        o = og_ref[:, vc]
        ms = jnp.mean(o * o, axis=-1, keepdims=True)
        on = o * lax.rsqrt(ms + RMS_EPS) * gn_ref[...]
        g = g_ref[:, vc].astype(F32)
        o_ref[:, vc] = (on * (g * _sigmoid(g))).astype(o_ref.dtype)

    @pl.when(t == nt - 1)
    def _():
        for h in range(GLA_HEADS):
            sf_ref[0, h] = st_ref[h].T[:GLA_DK, :]


def _gla(proj, w_lr2p, b_gatep, g_norm, s0, *, batch, seq):
    c = CHUNK if seq % CHUNK == 0 else seq
    tt = _tile(seq, 512)
    if tt % c:
        tt = c
    nt = seq // tt
    sums_np, pairs_np, nlev = _gla_tables(c)
    sums = jnp.asarray(sums_np, BF16)
    pairs = jnp.asarray(pairs_np, F32)
    nrow = sums_np.shape[0]
    kw = GLA_HEADS * GLA_DKP
    state_blk = (1, GLA_HEADS, GLA_DK, GLA_DV)
    state_vmem = (GLA_HEADS, GLA_DV, GLA_DKP)

    in_specs = [
        pl.BlockSpec((tt, MAIN_WIDTH), lambda b, t: (b * nt + t, A_OFF_V // MAIN_WIDTH)),
        pl.BlockSpec((tt, MAIN_WIDTH), lambda b, t: (b * nt + t, A_OFF_G // MAIN_WIDTH)),
        pl.BlockSpec((tt, kw), lambda b, t: (b * nt + t, A_OFF_Q // kw)),
        pl.BlockSpec((tt, kw), lambda b, t: (b * nt + t, A_OFF_K // kw)),
        pl.BlockSpec((tt, GLA_LRP), lambda b, t: (b * nt + t, A_OFF_LR // GLA_LRP)),
        pl.BlockSpec((GLA_LRP, kw), lambda b, t: (0, 0)),
        pl.BlockSpec((1, kw), lambda b, t: (0, 0)),
        pl.BlockSpec((1, GLA_DV), lambda b, t: (0, 0)),
        pl.BlockSpec((nrow, c), lambda b, t: (0, 0)),
        pl.BlockSpec((nlev + 1, c, c), lambda b, t: (0, 0, 0)),
        pl.BlockSpec(state_blk, lambda b, t: (b, 0, 0, 0)),
    ]
    blocks = (3 * _nbytes((tt, MAIN_WIDTH), BF16) + 2 * _nbytes((tt, kw), BF16) + _nbytes((tt, GLA_LRP), BF16)
              + _nbytes((GLA_LRP, kw), BF16) + 2 * _nbytes(state_blk, F32) + _nbytes((nrow, c), BF16)
              + _nbytes((nlev + 1, c, c), F32))
    return pl.pallas_call(
        functools.partial(_gla_kernel, c=c, nchunk=tt // c, nlev=nlev, nt=nt),
        grid=(batch, nt),
        in_specs=in_specs,
        out_specs=[pl.BlockSpec((tt, MAIN_WIDTH), lambda b, t: (b * nt + t, 0)),
                   pl.BlockSpec(state_blk, lambda b, t: (b, 0, 0, 0))],
        out_shape=[jax.ShapeDtypeStruct((batch * seq, MAIN_WIDTH), BF16),
                   jax.ShapeDtypeStruct((batch, GLA_HEADS, GLA_DK, GLA_DV), F32)],
        scratch_shapes=[pltpu.VMEM(state_vmem, F32), pltpu.VMEM((tt, kw), BF16),
                        pltpu.VMEM((tt, MAIN_WIDTH), F32)],
        compiler_params=pltpu.CompilerParams(
            dimension_semantics=("arbitrary", "arbitrary"),
            vmem_limit_bytes=_vmem_limit(
                blocks, temp_bytes=8 << 20,
                scratch_bytes=_nbytes(state_vmem, F32) + _nbytes((tt, kw), BF16) + _nbytes((tt, MAIN_WIDTH), F32))),
        name="gla_scan",
    )(proj, proj, proj, proj, proj, w_lr2p, b_gatep, g_norm, sums, pairs, s0)


def _sb_kernel(q_ref, k_ref, v_ref, u_ref, o_ref, acc_ref, carry_ref, *, tq, tk, q_off, nh):
    qi = pl.program_id(2)
    jd = (q_off + qi * tq) // tk
    q_pos = q_off + qi * tq + lax.broadcasted_iota(jnp.int32, (tq, tk), 0)
    ntile = tk // V7X_LANES

    def tile(h, j, carry, reach):
        cols = slice(h * HEAD_DIM, (h + 1) * HEAD_DIM)
        keys = pl.ds(pl.multiple_of(j * tk, tk), tk)
        z = _dot_nt(q_ref[:, cols], k_ref[keys, cols])
        sp = jnp.maximum(z, 0.0) + jnp.log2(1.0 + jnp.exp2(-jnp.abs(z)))
        if reach is not None:
            sp = jnp.where(reach, sp, 0.0)
        cs = _dot(sp.astype(BF16), u_ref[...])
        w = jnp.exp2(z - cs - jnp.concatenate([carry] * ntile, axis=1))
        if reach is not None:
            w = jnp.where(reach, w, 0.0)
        return _dot(w.astype(BF16), v_ref[keys, cols]), jnp.sum(sp, axis=1, keepdims=True)

    def live(carries):
        low = functools.reduce(jnp.minimum, carries)
        return jnp.min(low, axis=0, keepdims=True)[0, 0] < SB_DEAD_LOG2

    k_col = lax.broadcasted_iota(jnp.int32, (tq, tk), 1)
    diag_reach = (jd * tk + k_col) < q_pos
    has_prev = jnp.where(jd >= 1, 1.0, 0.0)
    carries = []
    for h in range(nh):
        a0, r0 = tile(h, jd, jnp.zeros((tq, V7X_LANES), F32), diag_reach)
        c0 = jnp.broadcast_to(r0, (tq, V7X_LANES))
        a1, r1 = tile(h, jnp.maximum(jd - 1, 0), c0, None)
        c1 = c0 + r1 * has_prev
        acc_ref[h] = a0 + a1 * has_prev
        carry_ref[h] = c1
        carries.append(c1)

    def cond(state):
        j, alive = state
        return jnp.logical_and(j >= 0, alive)

    def body(state):
        j, _ = state
        new = []
        for h in range(nh):
            carry = carry_ref[h]
            a, r = tile(h, j, carry, None)
            acc_ref[h] += a
            carry_ref[h] = carry + r
            new.append(carry + r)
        return j - 1, live(new)

    lax.while_loop(cond, body, (jd - 2, live(carries)))
    for h in range(nh):
        o_ref[:, h * HEAD_DIM:(h + 1) * HEAD_DIM] = acc_ref[h].astype(o_ref.dtype)


def _sb_attention(q_arr, q_col0, k_arr, v_arr, *, batch, seq_q, seq_k, q_off):
    tq = min(SB_TILE, seq_q)
    tk = SB_TILE
    nh = SB_HEADS_PER_STEP
    width = nh * HEAD_DIM
    assert seq_q % tq == 0 and seq_k % tk == 0 and q_off % tk == 0 and tq <= tk
    assert q_col0 % width == 0 and SB_HEADS % nh == 0
    nq = seq_q // tq
    u = jnp.asarray(np.tril(np.ones((tk, tk), np.float32)), BF16)
    blocks = 2 * _nbytes((tq, width), BF16) + 2 * _nbytes((seq_k, width), BF16) + _nbytes((tk, tk), BF16)
    return pl.pallas_call(
        functools.partial(_sb_kernel, tq=tq, tk=tk, q_off=q_off, nh=nh),
        grid=(batch, SB_HEADS // nh, nq),
        in_specs=[pl.BlockSpec((tq, width), lambda b, h, i: (b * nq + i, q_col0 // width + h)),
                  pl.BlockSpec((seq_k, width), lambda b, h, i: (b, h)),
                  pl.BlockSpec((seq_k, width), lambda b, h, i: (b, h)),
                  pl.BlockSpec((tk, tk), lambda b, h, i: (0, 0))],
        out_specs=pl.BlockSpec((tq, width), lambda b, h, i: (b * nq + i, h)),
        out_shape=jax.ShapeDtypeStruct((batch * seq_q, MAIN_WIDTH), BF16),
        scratch_shapes=[pltpu.VMEM((nh, tq, HEAD_DIM), F32), pltpu.VMEM((nh, tq, V7X_LANES), F32)],
        compiler_params=pltpu.CompilerParams(
            dimension_semantics=("arbitrary", "arbitrary", "arbitrary"),
            vmem_limit_bytes=_vmem_limit(blocks, temp_bytes=12 * nh * _nbytes((tq, tk), F32))),
        name="stick_breaking",
    )(q_arr, k_arr, v_arr, u)


def _cache_join_kernel(ck_hbm, cv_hbm, nk_ref, nv_ref, ok_ref, ov_ref, stage_ref, sem_ref, *, past):
    b = pl.program_id(0)
    srcs, news, outs = (ck_hbm, cv_hbm), (nk_ref, nv_ref), (ok_ref, ov_ref)

    def head_copy(r, h):
        return pltpu.make_async_copy(srcs[r].at[b, :, h, :], stage_ref.at[r, h % 2], sem_ref.at[r, h % 2])

    for r in range(2):
        head_copy(r, 0).start()
    for h in range(SB_HEADS):
        for r in range(2):
            if h + 1 < SB_HEADS:
                head_copy(r, h + 1).start()
            head_copy(r, h).wait()
            outs[r][0:past, h * HEAD_DIM:(h + 1) * HEAD_DIM] = stage_ref[r, h % 2].astype(outs[r].dtype)
    for r in range(2):
        pad_rows = outs[r].shape[0] - past - news[r].shape[0]
        zeros = jnp.zeros((pad_rows, outs[r].shape[1]), outs[r].dtype)
        outs[r][past:, :] = jnp.concatenate([news[r][...], zeros], axis=0)


def _cache_join(cache_k, cache_v, new_k, new_v):
    bs, past, nheads, hd = cache_k.shape
    new_len = new_k.shape[0] // bs
    width = nheads * hd
    lk_pad = past + SB_TILE
    assert past % SB_TILE == 0 and new_len <= SB_TILE and new_len % 16 == 0 and (nheads, hd) == (SB_HEADS, HEAD_DIM)
    new_spec = pl.BlockSpec((new_len, width), lambda b: (b, 0))
    out_spec = pl.BlockSpec((lk_pad, width), lambda b: (b, 0))
    out_sds = jax.ShapeDtypeStruct((bs * lk_pad, width), BF16)
    stage = (2, 2, past, hd)
    blocks = 2 * _nbytes((new_len, width), BF16) + 2 * _nbytes((lk_pad, width), BF16)
    return pl.pallas_call(
        functools.partial(_cache_join_kernel, past=past),
        grid=(bs,),
        in_specs=[pl.BlockSpec(memory_space=pl.ANY), pl.BlockSpec(memory_space=pl.ANY), new_spec, new_spec],
        out_specs=[out_spec, out_spec],
        out_shape=[out_sds, out_sds],
        scratch_shapes=[pltpu.VMEM(stage, F32), pltpu.SemaphoreType.DMA((2, 2))],
        compiler_params=pltpu.CompilerParams(
            dimension_semantics=("arbitrary",),
            vmem_limit_bytes=_vmem_limit(blocks, scratch_bytes=_nbytes(stage, F32),
                                         temp_bytes=2 * _nbytes((past, hd), F32))),
        name="cache_join",
    )(cache_k, cache_v, new_k, new_v)


def _mem_kernel(qm_ref, mk_ref, mv_ref, o_ref):
    for h in range(MEM_HEADS):
        cols = slice(h * HEAD_DIM, (h + 1) * HEAD_DIM)
        s = _dot_nt(qm_ref[:, cols], mk_ref[0, :, cols])
        e = jnp.exp(s - jnp.max(s, axis=-1, keepdims=True))
        den = jnp.sum(e, axis=-1, keepdims=True)
        o = _dot(e.astype(BF16), mv_ref[0, :, cols]) / den
        o_ref[:, cols] = o.astype(o_ref.dtype)


def _mem_attention(q_arr, q_col0, mk, mv, *, batch, seq):
    tt = _tile(seq, 512)
    nt = seq // tt
    mlen = mk.shape[1]
    assert q_col0 % MEM_WIDTH == 0
    blocks = 2 * _nbytes((tt, MEM_WIDTH), BF16) + 2 * _nbytes((mlen, MEM_WIDTH), BF16)
    return pl.pallas_call(
        _mem_kernel,
        grid=(batch, nt),
        in_specs=[pl.BlockSpec((tt, MEM_WIDTH), lambda b, t: (b * nt + t, q_col0 // MEM_WIDTH)),
                  pl.BlockSpec((1, mlen, MEM_WIDTH), lambda b, t: (b, 0, 0)),
                  pl.BlockSpec((1, mlen, MEM_WIDTH), lambda b, t: (b, 0, 0))],
        out_specs=pl.BlockSpec((tt, MEM_WIDTH), lambda b, t: (b * nt + t, 0)),
        out_shape=jax.ShapeDtypeStruct((batch * seq, MEM_WIDTH), BF16),
        compiler_params=pltpu.CompilerParams(
            dimension_semantics=("arbitrary", "arbitrary"),
            vmem_limit_bytes=_vmem_limit(blocks, temp_bytes=6 * _nbytes((tt, mlen), F32))),
        name="mem_attention",
    )(q_arr, mk, mv)


def _pad_heads(w, width, padded):
    lead = w.shape[:-1]
    w = w.reshape(lead + (GLA_HEADS, width))
    w = jnp.pad(w, [(0, 0)] * len(lead) + [(0, 0), (0, padded - width)])
    return w.reshape(lead + (GLA_HEADS * padded,))


def _layout_in_a(w_in):
    q = w_in[:, :GLA_KW]
    k = w_in[:, GLA_KW:2 * GLA_KW]
    v = w_in[:, 2 * GLA_KW:2 * GLA_KW + MAIN_WIDTH]
    g = w_in[:, 2 * GLA_KW + MAIN_WIDTH:2 * GLA_KW + 2 * MAIN_WIDTH]
    lr = w_in[:, 2 * GLA_KW + 2 * MAIN_WIDTH:2 * GLA_KW + 2 * MAIN_WIDTH + GLA_LOWRANK]
    qm = w_in[:, 2 * GLA_KW + 2 * MAIN_WIDTH + GLA_LOWRANK:]
    lr = jnp.pad(lr, ((0, 0), (0, GLA_LRP - GLA_LOWRANK)))
    w = jnp.concatenate([v, g, _pad_heads(q, GLA_DK, GLA_DKP), _pad_heads(k, GLA_DK, GLA_DKP), qm, lr], axis=1)
    scale = np.ones((1, A_WIDTH), np.float32)
    scale[:, A_OFF_Q:A_OFF_K] = GLA_DK ** -0.5
    scale[:, A_OFF_QM:A_OFF_LR] = MEM_SCALE
    return w.astype(BF16), jnp.asarray(scale)


def _scale_in_b():
    scale = np.ones((1, D_MODEL), np.float32)
    scale[:, :MAIN_WIDTH] = SB_SCALE * LOG2_E
    scale[:, MAIN_WIDTH:] = MEM_SCALE
    return jnp.asarray(scale)


def _finish_layer(x, o_main, o_mem, lw, shared):
    l = lw["layer"]
    x1, x1b = _matmul_residual_ln([o_main, o_mem], shared["w_o"], l, x, lw["ln1_g"], lw["ln1_b"], tm=512,
                                  nsplit=4)
    hid = _ffn_in(x1b, shared["w_ffn_in"], l)
    return _matmul_residual_ln([hid], shared["w_ffn_out"], l, x1, lw["ln2_g"], lw["ln2_b"], tm=256)


def _layer_a(x, xb, lw, shared, mk, mv, s0, *, batch, seq):
    proj, = _matmul(xb, lw["w_in"], out_dtypes=(BF16,), col_scale=lw["in_scale"], tm=1024, tn=1920)
    o_main, s_final = _gla(proj, lw["w_lr2p"], lw["b_gatep"], lw["g_norm"], s0, batch=batch, seq=seq)
    o_mem = _mem_attention(proj, A_OFF_QM, mk, mv, batch=batch, seq=seq)
    x, xb = _finish_layer(x, o_main, o_mem, lw, shared)
    return x, xb, s_final


def _layer_b(x, xb, lw, shared, mk, mv, kb, vb, *, batch, seq, seq_k, q_off):
    proj, = _matmul(xb, shared["w_in_b"], layer=lw["layer"] - N_A, out_dtypes=(BF16,),
                    col_scale=shared["in_b_scale"], tm=1024, tn=1024)
    o_main = _sb_attention(proj, 0, kb, vb, batch=batch, seq_q=seq, seq_k=seq_k, q_off=q_off)
    o_mem = _mem_attention(proj, MAIN_WIDTH, mk, mv, batch=batch, seq=seq)
    return _finish_layer(x, o_main, o_mem, lw, shared)


def kernel(x_prompt, x_sample, mem_prompt, state_gla, cache_sb_k, cache_sb_v, cache_mem_k, cache_mem_v,
           w_in_a, w_gate_lr, b_gate, gla_norm_g, w_in_b, w_kv_shared, w_mem_kv, w_o, ln1_g, ln1_b,
           ln2_g, ln2_b, w_ffn_in, w_ffn_out):
    bp, lp, _ = x_prompt.shape
    bs, ls, _ = x_sample.shape
    mlen = mem_prompt.shape[1]

    shared = {
        "w_o": w_o.astype(BF16),
        "w_ffn_in": w_ffn_in,
        "w_ffn_out": w_ffn_out.astype(BF16),
        "w_in_b": w_in_b,
        "in_b_scale": _scale_in_b(),
    }
    layers = []
    for l in range(DEPTH):
        lw = {
            "layer": l,
            "ln1_g": ln1_g[l][None, :], "ln1_b": ln1_b[l][None, :],
            "ln2_g": ln2_g[l][None, :], "ln2_b": ln2_b[l][None, :],
        }
        if l < N_A:
            lw["w_in"], lw["in_scale"] = _layout_in_a(w_in_a[l])
            lr2 = _pad_heads(w_gate_lr[l], GLA_DK, GLA_DKP)
            lw["w_lr2p"] = jnp.pad(lr2, ((0, GLA_LRP - GLA_LOWRANK), (0, 0))).astype(BF16)
            lw["b_gatep"] = _pad_heads(b_gate[l][None, :], GLA_DK, GLA_DKP)
            lw["g_norm"] = gla_norm_g[l][None, :]
        layers.append(lw)
    w_kv = w_kv_shared
    w_memkv = w_mem_kv

    xp = x_prompt.reshape(bp * lp, D_MODEL)
    xs = x_sample.reshape(bs * ls, D_MODEL)
    xpb, xsb = xp, xs
    memb = mem_prompt.reshape(bp * mlen, D_MODEL).astype(BF16)
    cmk = cache_mem_k.reshape(DEPTH, bs, mlen, MEM_WIDTH).astype(BF16)
    cmv = cache_mem_v.reshape(DEPTH, bs, mlen, MEM_WIDTH).astype(BF16)

    s_zero = jnp.zeros((bp, GLA_HEADS, GLA_DK, GLA_DV), F32)
    gla_p, gla_s, mk_list, mv_list = [], [], [], []
    for l in range(DEPTH):
        lw = layers[l]
        mkv, = _matmul(memb, w_memkv, layer=l, out_dtypes=(F32,), tm=1024, tn=1024)
        mk_p, mv_p = mkv[:, :MEM_WIDTH], mkv[:, MEM_WIDTH:]
        mk_list.append(mk_p.reshape(bp, mlen, MEM_HEADS, HEAD_DIM))
        mv_list.append(mv_p.reshape(bp, mlen, MEM_HEADS, HEAD_DIM))
        mk_pb = mk_p.astype(BF16).reshape(bp, mlen, MEM_WIDTH)
        mv_pb = mv_p.astype(BF16).reshape(bp, mlen, MEM_WIDTH)
        if l < N_A:
            xp, xpb, sp = _layer_a(xp, xpb, lw, shared, mk_pb, mv_pb, s_zero, batch=bp, seq=lp)
            xs, xsb, ss = _layer_a(xs, xsb, lw, shared, cmk[l], cmv[l], state_gla[l], batch=bs, seq=ls)
            gla_p.append(sp)
            gla_s.append(ss)
        else:
            if l == N_A:
                kp_f, kp_b = _kv_proj_dma(xpb, w_kv, 0, batch=bp, seq=lp)
                vp_f, vp_b = _kv_proj_dma(xpb, w_kv, 1, batch=bp, seq=lp)
                ks_f, ks_b = _kv_proj(xsb, w_kv, 0, batch=bs, seq=ls)
                vs_f, vs_b = _kv_proj(xsb, w_kv, 1, batch=bs, seq=ls)
                lk_pad = PAST_LEN + SB_TILE
                k_all_b, v_all_b = _cache_join(cache_sb_k, cache_sb_v, ks_b, vs_b)
            xp, xpb = _layer_b(xp, xpb, lw, shared, mk_pb, mv_pb, kp_b, vp_b, batch=bp, seq=lp, seq_k=lp,
                               q_off=0)
            xs, xsb = _layer_b(xs, xsb, lw, shared, cmk[l], cmv[l], k_all_b, v_all_b, batch=bs, seq=ls,
                               seq_k=lk_pad, q_off=PAST_LEN)

    return (xp.reshape(bp, lp, D_MODEL), xs.reshape(bs, ls, D_MODEL),
            jnp.stack(gla_p, 0), jnp.stack(gla_s, 0), kp_f, vp_f, ks_f, vs_f,
            jnp.stack(mk_list, 0), jnp.stack(mv_list, 0))
```

```python
import functools
import math

import numpy as np
import jax
import jax.numpy as jnp
from jax import lax
from jax.experimental import pallas as pl
from jax.experimental.pallas import tpu as pltpu

D_MODEL = 2048
DEPTH = 4
PAST_LEN = 2048
CHUNK = 64
N_A = DEPTH // 2
HEAD_DIM = 128
MEM_HEADS = 4
MEM_WIDTH = MEM_HEADS * HEAD_DIM
MAIN_WIDTH = D_MODEL - MEM_WIDTH
GLA_HEADS = 4
GLA_DV = MAIN_WIDTH // GLA_HEADS
GLA_DK = GLA_DV // 2
GLA_KW = GLA_HEADS * GLA_DK
GLA_LOWRANK = 16
GLA_TAU = 16.0
SB_HEADS = MAIN_WIDTH // HEAD_DIM
D_FF = ((8 * D_MODEL // 3 + 255) // 256) * 256
DN_ALPHA = (2 * DEPTH) ** 0.25
LN_EPS = 1e-5
RMS_EPS = 1e-6
SB_SCALE = HEAD_DIM ** -0.5
MEM_SCALE = HEAD_DIM ** -0.5

V7X_LANES = 128
V7X_MXU_DIM = 256
V7X_VMEM_BYTES = 64 * 1024 * 1024

GLA_DKP = V7X_MXU_DIM
GLA_LRP = V7X_LANES
A_OFF_V = 0
A_OFF_G = MAIN_WIDTH
A_OFF_Q = 2 * MAIN_WIDTH
A_OFF_K = A_OFF_Q + GLA_HEADS * GLA_DKP
A_OFF_QM = A_OFF_K + GLA_HEADS * GLA_DKP
A_OFF_LR = A_OFF_QM + MEM_WIDTH
A_WIDTH = A_OFF_LR + GLA_LRP

SB_TILE = 256
SB_HEADS_PER_STEP = 4
SB_DEAD_LOG2 = 150.0
LOG2_E = math.log2(math.e)

F32 = jnp.float32
BF16 = jnp.bfloat16


def _vmem_limit(block_bytes, scratch_bytes=0, temp_bytes=0):
    need = 2 * block_bytes + scratch_bytes + temp_bytes + (2 << 20)
    return int(min(max(need, 16 << 20), V7X_VMEM_BYTES - (8 << 20)))


def _nbytes(shape, dtype):
    return int(np.prod(shape)) * jnp.dtype(dtype).itemsize


def _tile(n, pref):
    if n <= pref:
        return n
    t = pref
    while n % t:
        t -= 1
    return t


def _dot(a, b):
    return jnp.dot(a, b, preferred_element_type=F32)


def _dot_nt(a, b):
    return lax.dot_general(a, b, (((1,), (1,)), ((), ())), preferred_element_type=F32)


def _dot_tn(a, b):
    return lax.dot_general(a, b, (((0,), (0,)), ((), ())), preferred_element_type=F32)


def _sigmoid(x):
    return 1.0 / (1.0 + jnp.exp(-x))


def _softplus(x):
    return jnp.maximum(x, 0.0) + jnp.log(1.0 + jnp.exp(-jnp.abs(x)))


def _mm_kernel(*refs, has_scale):
    if has_scale:
        a_ref, w_ref, s_ref = refs[:3]
        o_refs = refs[3:]
    else:
        a_ref, w_ref = refs[:2]
        o_refs = refs[2:]
    acc = _dot(a_ref[...].astype(BF16), w_ref[...].astype(BF16))
    if has_scale:
        acc = acc * s_ref[...]
    for o_ref in o_refs:
        o_ref[...] = acc.astype(o_ref.dtype)


def _weight_spec(w, block, index_map, layer, **kwargs):
    if w.ndim == 2:
        return pl.BlockSpec(block, index_map, **kwargs)
    return pl.BlockSpec((None,) + block, lambda *g: (layer,) + index_map(*g), **kwargs)


def _matmul(a, w, *, out_dtypes, layer=None, col_scale=None, tm=1024, tn=1024):
    m, k = a.shape
    n = w.shape[-1]
    tm, tn = _tile(m, tm), _tile(n, tn)
    in_specs = [pl.BlockSpec((tm, k), lambda j, i: (i, 0)),
                _weight_spec(w, (k, tn), lambda j, i: (0, j), layer)]
    args = [a, w]
    if col_scale is not None:
        in_specs.append(pl.BlockSpec((1, tn), lambda j, i: (0, j)))
        args.append(col_scale)
    blocks = _nbytes((tm, k), a.dtype) + _nbytes((k, tn), w.dtype) + sum(_nbytes((tm, tn), d) for d in out_dtypes)
    outs = pl.pallas_call(
        functools.partial(_mm_kernel, has_scale=col_scale is not None),
        grid=(n // tn, m // tm),
        in_specs=in_specs,
        out_specs=[pl.BlockSpec((tm, tn), lambda j, i: (i, j)) for _ in out_dtypes],
        out_shape=[jax.ShapeDtypeStruct((m, n), d) for d in out_dtypes],
        compiler_params=pltpu.CompilerParams(
            dimension_semantics=("arbitrary", "arbitrary"),
            vmem_limit_bytes=_vmem_limit(blocks, temp_bytes=2 * _nbytes((tm, tn), F32))),
        name="proj_matmul",
    )(*args)
    return outs


def _swiglu_kernel(a_ref, wg_ref, wu_ref, o_ref, *, nsplit):
    rows_per = a_ref.shape[0] // nsplit
    wg = wg_ref[...].astype(BF16)
    wu = wu_ref[...].astype(BF16)
    for r in range(nsplit):
        rows = slice(r * rows_per, (r + 1) * rows_per)
        a = a_ref[rows, :]
        gate = _dot(a, wg)
        up = _dot(a, wu)
        o_ref[rows, :] = (gate * _sigmoid(gate) * up).astype(o_ref.dtype)


def _ffn_in(a, w, layer, *, tm=2048, tn=512):
    m, k = a.shape
    tm, tn = _tile(m, tm), _tile(D_FF, tn)
    nj = D_FF // tn
    blocks = _nbytes((tm, k), BF16) + 2 * _nbytes((k, tn), w.dtype) + _nbytes((tm, tn), BF16)
    return pl.pallas_call(
        functools.partial(_swiglu_kernel, nsplit=4 if tm % 1024 == 0 else 1),
        grid=(nj, m // tm),
        in_specs=[pl.BlockSpec((tm, k), lambda j, i: (i, 0)),
                  _weight_spec(w, (k, tn), lambda j, i: (0, j), layer),
                  _weight_spec(w, (k, tn), lambda j, i: (0, j + nj), layer)],
        out_specs=pl.BlockSpec((tm, tn), lambda j, i: (i, j)),
        out_shape=jax.ShapeDtypeStruct((m, D_FF), BF16),
        compiler_params=pltpu.CompilerParams(
            dimension_semantics=("arbitrary", "arbitrary"),
            vmem_limit_bytes=_vmem_limit(blocks, temp_bytes=4 * _nbytes((tm, tn), F32))),
        name="ffn_in_swiglu",
    )(a, w, w)


def _ln_kernel(*refs, nparts, nsplit):
    a_refs = refs[:nparts]
    w_ref, x_ref, g_ref, b_ref, of_ref, ob_ref = refs[nparts:]
    rows_per = x_ref.shape[0] // nsplit
    for r in range(nsplit):
        rows = slice(r * rows_per, (r + 1) * rows_per)
        a = a_refs[0][rows, :] if nparts == 1 else jnp.concatenate([p[rows, :] for p in a_refs], axis=1)
        y = DN_ALPHA * x_ref[rows, :] + _dot(a, w_ref[...])
        mu = jnp.mean(y, axis=-1, keepdims=True)
        d = y - mu
        var = jnp.mean(d * d, axis=-1, keepdims=True)
        out = d * lax.rsqrt(var + LN_EPS) * g_ref[...] + b_ref[...]
        of_ref[rows, :] = out
        ob_ref[rows, :] = out.astype(ob_ref.dtype)


def _matmul_residual_ln(a_parts, w, layer, x, gamma, beta, *, tm, nsplit=1):
    m, n = x.shape
    k = w.shape[-2]
    assert sum(a.shape[1] for a in a_parts) == k
    tm = _tile(m, tm)
    assert tm % nsplit == 0
    in_specs = [pl.BlockSpec((tm, a.shape[1]), lambda i: (i, 0)) for a in a_parts]
    in_specs += [_weight_spec(w, (k, n), lambda i: (0, 0), layer, pipeline_mode=pl.Buffered(1)),
                 pl.BlockSpec((tm, n), lambda i: (i, 0)),
                 pl.BlockSpec((1, n), lambda i: (0, 0)),
                 pl.BlockSpec((1, n), lambda i: (0, 0))]
    args = list(a_parts) + [w]
    resident = _nbytes((k, n), BF16)
    blocks = _nbytes((tm, k), BF16) + 2 * _nbytes((tm, n), F32) + _nbytes((tm, n), BF16)
    return pl.pallas_call(
        functools.partial(_ln_kernel, nparts=len(a_parts), nsplit=nsplit),
        grid=(m // tm,),
        in_specs=in_specs,
        out_specs=[pl.BlockSpec((tm, n), lambda i: (i, 0)),
                   pl.BlockSpec((tm, n), lambda i: (i, 0))],
        out_shape=[jax.ShapeDtypeStruct((m, n), F32), jax.ShapeDtypeStruct((m, n), BF16)],
        compiler_params=pltpu.CompilerParams(
            dimension_semantics=("arbitrary",),
            vmem_limit_bytes=_vmem_limit(blocks, scratch_bytes=resident,
                                         temp_bytes=3 * _nbytes((tm, n), F32))),
        name="matmul_residual_ln",
    )(*args, x, gamma, beta)


def _kv_kernel(a_ref, w_ref, of_ref, ob_ref):
    acc = _dot(a_ref[...], w_ref[...].astype(BF16))
    ob_ref[...] = acc.astype(ob_ref.dtype)
    for h in range(SB_HEADS):
        of_ref[0, :, h, :] = acc[:, h * HEAD_DIM:(h + 1) * HEAD_DIM]


def _kv_dma_kernel(a_ref, w_ref, of_hbm, ob_ref, stage_ref, sem_ref, *, nb, nsteps):
    i = pl.program_id(0)
    tm = a_ref.shape[0]
    slot = i % 2

    def head_copy(step, sl, h):
        rows = pl.ds((step % nb) * tm, tm)
        return pltpu.make_async_copy(stage_ref.at[sl, :, pl.ds(h * HEAD_DIM, HEAD_DIM)],
                                     of_hbm.at[step // nb, rows, h, :], sem_ref.at[sl])

    def wait_step(step, sl):
        for h in range(SB_HEADS):
            head_copy(step, sl, h).wait()

    @pl.when(i >= 2)
    def _():
        wait_step(i - 2, slot)

    acc = _dot(a_ref[...], w_ref[...].astype(BF16))
    ob_ref[...] = acc.astype(ob_ref.dtype)
    stage_ref[slot] = acc
    for h in range(SB_HEADS):
        head_copy(i, slot, h).start()

    @pl.when(i == nsteps - 1)
    def _():
        if nsteps >= 2:
            wait_step(i - 1, 1 - slot)
        wait_step(i, slot)


def _kv_proj_dma(a, w, col_blk, *, batch, seq, tm=512):
    m, k = a.shape
    tm = _tile(seq, tm)
    nb = seq // tm
    nsteps = m // tm
    blocks = _nbytes((tm, k), BF16) + _nbytes((tm, MAIN_WIDTH), BF16)
    return pl.pallas_call(
        functools.partial(_kv_dma_kernel, nb=nb, nsteps=nsteps),
        grid=(nsteps,),
        in_specs=[pl.BlockSpec((tm, k), lambda i: (i, 0)),
                  pl.BlockSpec((k, MAIN_WIDTH), lambda i: (0, col_blk), pipeline_mode=pl.Buffered(1))],
        out_specs=[pl.BlockSpec(memory_space=pl.ANY),
                   pl.BlockSpec((tm, MAIN_WIDTH), lambda i: (i, 0))],
        out_shape=[jax.ShapeDtypeStruct((batch, seq, SB_HEADS, HEAD_DIM), F32),
                   jax.ShapeDtypeStruct((m, MAIN_WIDTH), BF16)],
        scratch_shapes=[pltpu.VMEM((2, tm, MAIN_WIDTH), F32), pltpu.SemaphoreType.DMA((2,))],
        compiler_params=pltpu.CompilerParams(
            dimension_semantics=("arbitrary",),
            vmem_limit_bytes=_vmem_limit(
                blocks, scratch_bytes=_nbytes((k, MAIN_WIDTH), w.dtype) + _nbytes((2, tm, MAIN_WIDTH), F32),
                temp_bytes=2 * _nbytes((tm, MAIN_WIDTH), F32))),
        name="kv_proj_dma",
    )(a, w)


def _kv_proj(a, w, col_blk, *, batch, seq, tm=512):
    m, k = a.shape
    tm = _tile(seq, tm)
    nb = seq // tm
    sublane_padded_heads = -(-SB_HEADS // 8) * 8
    blocks = (_nbytes((tm, k), BF16) + _nbytes((tm, MAIN_WIDTH), BF16)
              + _nbytes((tm, sublane_padded_heads, HEAD_DIM), F32))
    return pl.pallas_call(
        _kv_kernel,
        grid=(m // tm,),
        in_specs=[pl.BlockSpec((tm, k), lambda i: (i, 0)),
                  pl.BlockSpec((k, MAIN_WIDTH), lambda i: (0, col_blk), pipeline_mode=pl.Buffered(1))],
        out_specs=[pl.BlockSpec((1, tm, SB_HEADS, HEAD_DIM), lambda i: (i // nb, i % nb, 0, 0)),
                   pl.BlockSpec((tm, MAIN_WIDTH), lambda i: (i, 0))],
        out_shape=[jax.ShapeDtypeStruct((batch, seq, SB_HEADS, HEAD_DIM), F32),
                   jax.ShapeDtypeStruct((m, MAIN_WIDTH), BF16)],
        compiler_params=pltpu.CompilerParams(
            dimension_semantics=("arbitrary",),
            vmem_limit_bytes=_vmem_limit(blocks, scratch_bytes=_nbytes((k, MAIN_WIDTH), w.dtype),
                                         temp_bytes=2 * _nbytes((tm, MAIN_WIDTH), F32))),
        name="kv_proj",
    )(a, w)


def _gla_tables(c):
    nlev = int(math.log2(c))
    assert 1 << nlev == c
    idx = np.arange(c)
    i, t = idx[:, None], idx[None, :]
    sums, pairs = [], []
    for lev in range(nlev):
        h = c >> (lev + 1)
        p = (i // (2 * h)) * 2 * h + h - 1
        second = (i % (2 * h)) >= h
        sums.append(np.where(second, (t > p) & (t <= i), (t > i) & (t <= p)))
        pairs.append((i // (2 * h) == t // (2 * h)) & second & ((t % (2 * h)) < h))
    sums.append(t <= i)
    sums.append(t > i)
    pairs.append(i == t)
    return (np.concatenate(sums, 0).astype(np.float32), np.stack(pairs, 0).astype(np.float32), nlev)


def _gla_kernel(v_ref, g_ref, q_ref, k_ref, lr_ref, wlr_ref, bg_ref, gn_ref, sum_ref, pair_ref, s0_ref,
                o_ref, sf_ref, st_ref, la_ref, og_ref, *, c, nchunk, nlev, nt):
    t = pl.program_id(1)

    @pl.when(t == 0)
    def _():
        pad = jnp.zeros((GLA_DKP - GLA_DK, GLA_DV), F32)
        for h in range(GLA_HEADS):
            st_ref[h] = jnp.concatenate([s0_ref[0, h], pad], axis=0).T

    zg = _dot(lr_ref[...], wlr_ref[...]) + bg_ref[...]
    la_ref[...] = (-_softplus(-zg) * (1.0 / GLA_TAU)).astype(BF16)

    def head(h, rows):
        kc = slice(h * GLA_DKP, (h + 1) * GLA_DKP)
        vc = slice(h * GLA_DV, (h + 1) * GLA_DV)
        w = jnp.exp(_dot(sum_ref[...], la_ref[rows, kc]))
        qb = q_ref[rows, kc]
        kb = k_ref[rows, kc]
        vb = v_ref[rows, vc]
        q = qb.astype(F32)
        k = kb.astype(F32)
        att = pair_ref[nlev] * _dot_nt(qb, kb)
        for lev in range(nlev):
            wl = w[lev * c:(lev + 1) * c]
            att = att + pair_ref[lev] * _dot_nt((q * wl).astype(BF16), (k * wl).astype(BF16))
        w_pre = w[nlev * c:(nlev + 1) * c]
        w_suf = w[(nlev + 1) * c:(nlev + 2) * c]
        st = st_ref[h]
        og_ref[rows, vc] = _dot(att.astype(BF16), vb) + _dot_nt((q * w_pre).astype(BF16), st.astype(BF16))
        st_ref[h] = st * w_pre[c - 1:c, :] + _dot_tn(vb, (k * w_suf).astype(BF16))

    def chunk(ci, carry):
        rows = pl.ds(pl.multiple_of(ci * c, c), c)
        for h in range(GLA_HEADS):
            head(h, rows)
        return carry

    lax.fori_loop(0, nchunk, chunk, 0, unroll=min(2, nchunk))

    for h in range(GLA_HEADS):
        vc = slice(h * GLA_DV, (h + 1) * GLA_DV)
        o = og_ref[:, vc]
        ms = jnp.mean(o * o, axis=-1, keepdims=True)
        on = o * lax.rsqrt(ms + RMS_EPS) * gn_ref[...]
        g = g_ref[:, vc].astype(F32)
        o_ref[:, vc] = (on * (g * _sigmoid(g))).astype(o_ref.dtype)

    @pl.when(t == nt - 1)
    def _():
        for h in range(GLA_HEADS):
            sf_ref[0, h] = st_ref[h].T[:GLA_DK, :]


def _gla(proj, w_lr2p, b_gatep, g_norm, s0, *, batch, seq):
    c = CHUNK if seq % CHUNK == 0 else seq
    tt = _tile(seq, 512)
    if tt % c:
        tt = c
    nt = seq // tt
    sums_np, pairs_np, nlev = _gla_tables(c)
    sums = jnp.asarray(sums_np, BF16)
    pairs = jnp.asarray(pairs_np, F32)
    nrow = sums_np.shape[0]
    kw = GLA_HEADS * GLA_DKP
    state_blk = (1, GLA_HEADS, GLA_DK, GLA_DV)
    state_vmem = (GLA_HEADS, GLA_DV, GLA_DKP)

    in_specs = [
        pl.BlockSpec((tt, MAIN_WIDTH), lambda b, t: (b * nt + t, A_OFF_V // MAIN_WIDTH)),
        pl.BlockSpec((tt, MAIN_WIDTH), lambda b, t: (b * nt + t, A_OFF_G // MAIN_WIDTH)),
        pl.BlockSpec((tt, kw), lambda b, t: (b * nt + t, A_OFF_Q // kw)),
        pl.BlockSpec((tt, kw), lambda b, t: (b * nt + t, A_OFF_K // kw)),
        pl.BlockSpec((tt, GLA_LRP), lambda b, t: (b * nt + t, A_OFF_LR // GLA_LRP)),
        pl.BlockSpec((GLA_LRP, kw), lambda b, t: (0, 0)),
        pl.BlockSpec((1, kw), lambda b, t: (0, 0)),
        pl.BlockSpec((1, GLA_DV), lambda b, t: (0, 0)),
        pl.BlockSpec((nrow, c), lambda b, t: (0, 0)),
        pl.BlockSpec((nlev + 1, c, c), lambda b, t: (0, 0, 0)),
        pl.BlockSpec(state_blk, lambda b, t: (b, 0, 0, 0)),
    ]
    blocks = (3 * _nbytes((tt, MAIN_WIDTH), BF16) + 2 * _nbytes((tt, kw), BF16) + _nbytes((tt, GLA_LRP), BF16)
              + _nbytes((GLA_LRP, kw), BF16) + 2 * _nbytes(state_blk, F32) + _nbytes((nrow, c), BF16)
              + _nbytes((nlev + 1, c, c), F32))
    return pl.pallas_call(
        functools.partial(_gla_kernel, c=c, nchunk=tt // c, nlev=nlev, nt=nt),
        grid=(batch, nt),
        in_specs=in_specs,
        out_specs=[pl.BlockSpec((tt, MAIN_WIDTH), lambda b, t: (b * nt + t, 0)),
                   pl.BlockSpec(state_blk, lambda b, t: (b, 0, 0, 0))],
        out_shape=[jax.ShapeDtypeStruct((batch * seq, MAIN_WIDTH), BF16),
                   jax.ShapeDtypeStruct((batch, GLA_HEADS, GLA_DK, GLA_DV), F32)],
        scratch_shapes=[pltpu.VMEM(state_vmem, F32), pltpu.VMEM((tt, kw), BF16),
                        pltpu.VMEM((tt, MAIN_WIDTH), F32)],
        compiler_params=pltpu.CompilerParams(
            dimension_semantics=("arbitrary", "arbitrary"),
            vmem_limit_bytes=_vmem_limit(
                blocks, temp_bytes=8 << 20,
                scratch_bytes=_nbytes(state_vmem, F32) + _nbytes((tt, kw), BF16) + _nbytes((tt, MAIN_WIDTH), F32))),
        name="gla_scan",
    )(proj, proj, proj, proj, proj, w_lr2p, b_gatep, g_norm, sums, pairs, s0)


def _sb_kernel(q_ref, k_ref, v_ref, u_ref, o_ref, acc_ref, carry_ref, *, tq, tk, q_off, nh):
    qi = pl.program_id(2)
    jd = (q_off + qi * tq) // tk
    q_pos = q_off + qi * tq + lax.broadcasted_iota(jnp.int32, (tq, tk), 0)
    ntile = tk // V7X_LANES

    def tile(h, j, carry, reach):
        cols = slice(h * HEAD_DIM, (h + 1) * HEAD_DIM)
        keys = pl.ds(pl.multiple_of(j * tk, tk), tk)
        z = _dot_nt(q_ref[:, cols], k_ref[keys, cols])
        sp = jnp.maximum(z, 0.0) + jnp.log2(1.0 + jnp.exp2(-jnp.abs(z)))
        if reach is not None:
            sp = jnp.where(reach, sp, 0.0)
        cs = _dot(sp.astype(BF16), u_ref[...])
        w = jnp.exp2(z - cs - jnp.concatenate([carry] * ntile, axis=1))
        if reach is not None:
            w = jnp.where(reach, w, 0.0)
        return _dot(w.astype(BF16), v_ref[keys, cols]), jnp.sum(sp, axis=1, keepdims=True)

    def live(carries):
        low = functools.reduce(jnp.minimum, carries)
        return jnp.min(low, axis=0, keepdims=True)[0, 0] < SB_DEAD_LOG2

    k_col = lax.broadcasted_iota(jnp.int32, (tq, tk), 1)
    diag_reach = (jd * tk + k_col) < q_pos
    has_prev = jnp.where(jd >= 1, 1.0, 0.0)
    carries = []
    for h in range(nh):
        a0, r0 = tile(h, jd, jnp.zeros((tq, V7X_LANES), F32), diag_reach)
        c0 = jnp.broadcast_to(r0, (tq, V7X_LANES))
        a1, r1 = tile(h, jnp.maximum(jd - 1, 0), c0, None)
        c1 = c0 + r1 * has_prev
        acc_ref[h] = a0 + a1 * has_prev
        carry_ref[h] = c1
        carries.append(c1)

    def cond(state):
        j, alive = state
        return jnp.logical_and(j >= 0, alive)

    def body(state):
        j, _ = state
        new = []
        for h in range(nh):
            carry = carry_ref[h]
            a, r = tile(h, j, carry, None)
            acc_ref[h] += a
            carry_ref[h] = carry + r
            new.append(carry + r)
        return j - 1, live(new)

    lax.while_loop(cond, body, (jd - 2, live(carries)))
    for h in range(nh):
        o_ref[:, h * HEAD_DIM:(h + 1) * HEAD_DIM] = acc_ref[h].astype(o_ref.dtype)


def _sb_attention(q_arr, q_col0, k_arr, v_arr, *, batch, seq_q, seq_k, q_off):
    tq = min(SB_TILE, seq_q)
    tk = SB_TILE
    nh = SB_HEADS_PER_STEP
    width = nh * HEAD_DIM
    assert seq_q % tq == 0 and seq_k % tk == 0 and q_off % tk == 0 and tq <= tk
    assert q_col0 % width == 0 and SB_HEADS % nh == 0
    nq = seq_q // tq
    u = jnp.asarray(np.tril(np.ones((tk, tk), np.float32)), BF16)
    blocks = 2 * _nbytes((tq, width), BF16) + 2 * _nbytes((seq_k, width), BF16) + _nbytes((tk, tk), BF16)
    return pl.pallas_call(
        functools.partial(_sb_kernel, tq=tq, tk=tk, q_off=q_off, nh=nh),
        grid=(batch, SB_HEADS // nh, nq),
        in_specs=[pl.BlockSpec((tq, width), lambda b, h, i: (b * nq + i, q_col0 // width + h)),
                  pl.BlockSpec((seq_k, width), lambda b, h, i: (b, h)),
                  pl.BlockSpec((seq_k, width), lambda b, h, i: (b, h)),
                  pl.BlockSpec((tk, tk), lambda b, h, i: (0, 0))],
        out_specs=pl.BlockSpec((tq, width), lambda b, h, i: (b * nq + i, h)),
        out_shape=jax.ShapeDtypeStruct((batch * seq_q, MAIN_WIDTH), BF16),
        scratch_shapes=[pltpu.VMEM((nh, tq, HEAD_DIM), F32), pltpu.VMEM((nh, tq, V7X_LANES), F32)],
        compiler_params=pltpu.CompilerParams(
            dimension_semantics=("arbitrary", "arbitrary", "arbitrary"),
            vmem_limit_bytes=_vmem_limit(blocks, temp_bytes=12 * nh * _nbytes((tq, tk), F32))),
        name="stick_breaking",
    )(q_arr, k_arr, v_arr, u)


def _cache_join_kernel(ck_ref, cv_ref, nk_ref, nv_ref, ok_ref, ov_ref, *, past):
    for c_ref, n_ref, o_ref in ((ck_ref, nk_ref, ok_ref), (cv_ref, nv_ref, ov_ref)):
        o_ref[0:past, :] = c_ref[0, 0].astype(o_ref.dtype)
        zeros = jnp.zeros((o_ref.shape[0] - past - n_ref.shape[0], o_ref.shape[1]), o_ref.dtype)
        o_ref[past:, :] = jnp.concatenate([n_ref[...], zeros], axis=0)


def _cache_join(cache_k, cache_v, new_k, new_v):
    bs, past, nheads, hd = cache_k.shape
    new_len = new_k.shape[0] // bs
    lk_pad = past + SB_TILE
    assert past % SB_TILE == 0 and new_len <= SB_TILE and new_len % 16 == 0
    cache_spec = pl.BlockSpec((1, 1, past, hd), lambda b, h: (b, h, 0, 0))
    new_spec = pl.BlockSpec((new_len, hd), lambda b, h: (b, h))
    out_spec = pl.BlockSpec((lk_pad, hd), lambda b, h: (b, h))
    out_sds = jax.ShapeDtypeStruct((bs * lk_pad, nheads * hd), BF16)
    blocks = 2 * (_nbytes((past, hd), F32) + _nbytes((new_len, hd), BF16) + _nbytes((lk_pad, hd), BF16))
    return pl.pallas_call(
        functools.partial(_cache_join_kernel, past=past),
        grid=(bs, nheads),
        in_specs=[cache_spec, cache_spec, new_spec, new_spec],
        out_specs=[out_spec, out_spec],
        out_shape=[out_sds, out_sds],
        compiler_params=pltpu.CompilerParams(
            dimension_semantics=("arbitrary", "arbitrary"),
            vmem_limit_bytes=_vmem_limit(blocks)),
        name="cache_join",
    )(jnp.swapaxes(cache_k, 1, 2), jnp.swapaxes(cache_v, 1, 2), new_k, new_v)


def _mem_kernel(qm_ref, mk_ref, mv_ref, o_ref):
    for h in range(MEM_HEADS):
        cols = slice(h * HEAD_DIM, (h + 1) * HEAD_DIM)
        s = _dot_nt(qm_ref[:, cols], mk_ref[0, :, cols])
        e = jnp.exp(s - jnp.max(s, axis=-1, keepdims=True))
        den = jnp.sum(e, axis=-1, keepdims=True)
        o = _dot(e.astype(BF16), mv_ref[0, :, cols]) / den
        o_ref[:, cols] = o.astype(o_ref.dtype)


def _mem_attention(q_arr, q_col0, mk, mv, *, batch, seq):
    tt = _tile(seq, 512)
    nt = seq // tt
    mlen = mk.shape[1]
    assert q_col0 % MEM_WIDTH == 0
    blocks = 2 * _nbytes((tt, MEM_WIDTH), BF16) + 2 * _nbytes((mlen, MEM_WIDTH), BF16)
    return pl.pallas_call(
        _mem_kernel,
        grid=(batch, nt),
        in_specs=[pl.BlockSpec((tt, MEM_WIDTH), lambda b, t: (b * nt + t, q_col0 // MEM_WIDTH)),
                  pl.BlockSpec((1, mlen, MEM_WIDTH), lambda b, t: (b, 0, 0)),
                  pl.BlockSpec((1, mlen, MEM_WIDTH), lambda b, t: (b, 0, 0))],
        out_specs=pl.BlockSpec((tt, MEM_WIDTH), lambda b, t: (b * nt + t, 0)),
        out_shape=jax.ShapeDtypeStruct((batch * seq, MEM_WIDTH), BF16),
        compiler_params=pltpu.CompilerParams(
            dimension_semantics=("arbitrary", "arbitrary"),
            vmem_limit_bytes=_vmem_limit(blocks, temp_bytes=6 * _nbytes((tt, mlen), F32))),
        name="mem_attention",
    )(q_arr, mk, mv)


def _pad_heads(w, width, padded):
    lead = w.shape[:-1]
    w = w.reshape(lead + (GLA_HEADS, width))
    w = jnp.pad(w, [(0, 0)] * len(lead) + [(0, 0), (0, padded - width)])
    return w.reshape(lead + (GLA_HEADS * padded,))


def _layout_in_a(w_in):
    q = w_in[:, :GLA_KW]
    k = w_in[:, GLA_KW:2 * GLA_KW]
    v = w_in[:, 2 * GLA_KW:2 * GLA_KW + MAIN_WIDTH]
    g = w_in[:, 2 * GLA_KW + MAIN_WIDTH:2 * GLA_KW + 2 * MAIN_WIDTH]
    lr = w_in[:, 2 * GLA_KW + 2 * MAIN_WIDTH:2 * GLA_KW + 2 * MAIN_WIDTH + GLA_LOWRANK]
    qm = w_in[:, 2 * GLA_KW + 2 * MAIN_WIDTH + GLA_LOWRANK:]
    lr = jnp.pad(lr, ((0, 0), (0, GLA_LRP - GLA_LOWRANK)))
    w = jnp.concatenate([v, g, _pad_heads(q, GLA_DK, GLA_DKP), _pad_heads(k, GLA_DK, GLA_DKP), qm, lr], axis=1)
    scale = np.ones((1, A_WIDTH), np.float32)
    scale[:, A_OFF_Q:A_OFF_K] = GLA_DK ** -0.5
    scale[:, A_OFF_QM:A_OFF_LR] = MEM_SCALE
    return w.astype(BF16), jnp.asarray(scale)


def _scale_in_b():
    scale = np.ones((1, D_MODEL), np.float32)
    scale[:, :MAIN_WIDTH] = SB_SCALE * LOG2_E
    scale[:, MAIN_WIDTH:] = MEM_SCALE
    return jnp.asarray(scale)


def _finish_layer(x, o_main, o_mem, lw, shared):
    l = lw["layer"]
    x1, x1b = _matmul_residual_ln([o_main, o_mem], shared["w_o"], l, x, lw["ln1_g"], lw["ln1_b"], tm=512,
                                  nsplit=4)
    hid = _ffn_in(x1b, shared["w_ffn_in"], l)
    return _matmul_residual_ln([hid], shared["w_ffn_out"], l, x1, lw["ln2_g"], lw["ln2_b"], tm=256)


def _layer_a(x, xb, lw, shared, mk, mv, s0, *, batch, seq):
    proj, = _matmul(xb, lw["w_in"], out_dtypes=(BF16,), col_scale=lw["in_scale"], tm=1024, tn=1920)
    o_main, s_final = _gla(proj, lw["w_lr2p"], lw["b_gatep"], lw["g_norm"], s0, batch=batch, seq=seq)
    o_mem = _mem_attention(proj, A_OFF_QM, mk, mv, batch=batch, seq=seq)
    x, xb = _finish_layer(x, o_main, o_mem, lw, shared)
    return x, xb, s_final


def _layer_b(x, xb, lw, shared, mk, mv, kb, vb, *, batch, seq, seq_k, q_off):
    proj, = _matmul(xb, shared["w_in_b"], layer=lw["layer"] - N_A, out_dtypes=(BF16,),
                    col_scale=shared["in_b_scale"], tm=1024, tn=1024)
    o_main = _sb_attention(proj, 0, kb, vb, batch=batch, seq_q=seq, seq_k=seq_k, q_off=q_off)
    o_mem = _mem_attention(proj, MAIN_WIDTH, mk, mv, batch=batch, seq=seq)
    return _finish_layer(x, o_main, o_mem, lw, shared)


def kernel(x_prompt, x_sample, mem_prompt, state_gla, cache_sb_k, cache_sb_v, cache_mem_k, cache_mem_v,
           w_in_a, w_gate_lr, b_gate, gla_norm_g, w_in_b, w_kv_shared, w_mem_kv, w_o, ln1_g, ln1_b,
           ln2_g, ln2_b, w_ffn_in, w_ffn_out):
    bp, lp, _ = x_prompt.shape
    bs, ls, _ = x_sample.shape
    mlen = mem_prompt.shape[1]

    shared = {
        "w_o": w_o.astype(BF16),
        "w_ffn_in": w_ffn_in,
        "w_ffn_out": w_ffn_out.astype(BF16),
        "w_in_b": w_in_b,
        "in_b_scale": _scale_in_b(),
    }
    layers = []
    for l in range(DEPTH):
        lw = {
            "layer": l,
            "ln1_g": ln1_g[l][None, :], "ln1_b": ln1_b[l][None, :],
            "ln2_g": ln2_g[l][None, :], "ln2_b": ln2_b[l][None, :],
        }
        if l < N_A:
            lw["w_in"], lw["in_scale"] = _layout_in_a(w_in_a[l])
            lr2 = _pad_heads(w_gate_lr[l], GLA_DK, GLA_DKP)
            lw["w_lr2p"] = jnp.pad(lr2, ((0, GLA_LRP - GLA_LOWRANK), (0, 0))).astype(BF16)
            lw["b_gatep"] = _pad_heads(b_gate[l][None, :], GLA_DK, GLA_DKP)
            lw["g_norm"] = gla_norm_g[l][None, :]
        layers.append(lw)
    w_kv = w_kv_shared
    w_memkv = w_mem_kv

    xp = x_prompt.reshape(bp * lp, D_MODEL)
    xs = x_sample.reshape(bs * ls, D_MODEL)
    xpb, xsb = xp, xs
    memb = mem_prompt.reshape(bp * mlen, D_MODEL).astype(BF16)
    cmk = cache_mem_k.reshape(DEPTH, bs, mlen, MEM_WIDTH).astype(BF16)
    cmv = cache_mem_v.reshape(DEPTH, bs, mlen, MEM_WIDTH).astype(BF16)

    s_zero = jnp.zeros((bp, GLA_HEADS, GLA_DK, GLA_DV), F32)
    gla_p, gla_s, mk_list, mv_list = [], [], [], []
    for l in range(DEPTH):
        lw = layers[l]
        mkv, = _matmul(memb, w_memkv, layer=l, out_dtypes=(F32,), tm=1024, tn=1024)
        mk_p, mv_p = mkv[:, :MEM_WIDTH], mkv[:, MEM_WIDTH:]
        mk_list.append(mk_p.reshape(bp, mlen, MEM_HEADS, HEAD_DIM))
        mv_list.append(mv_p.reshape(bp, mlen, MEM_HEADS, HEAD_DIM))
        mk_pb = mk_p.astype(BF16).reshape(bp, mlen, MEM_WIDTH)
        mv_pb = mv_p.astype(BF16).reshape(bp, mlen, MEM_WIDTH)
        if l < N_A:
            xp, xpb, sp = _layer_a(xp, xpb, lw, shared, mk_pb, mv_pb, s_zero, batch=bp, seq=lp)
            xs, xsb, ss = _layer_a(xs, xsb, lw, shared, cmk[l], cmv[l], state_gla[l], batch=bs, seq=ls)
            gla_p.append(sp)
            gla_s.append(ss)
        else:
            if l == N_A:
                kp_f, kp_b = _kv_proj_dma(xpb, w_kv, 0, batch=bp, seq=lp)
                vp_f, vp_b = _kv_proj_dma(xpb, w_kv, 1, batch=bp, seq=lp)
                ks_f, ks_b = _kv_proj(xsb, w_kv, 0, batch=bs, seq=ls)
                vs_f, vs_b = _kv_proj(xsb, w_kv, 1, batch=bs, seq=ls)
                lk_pad = PAST_LEN + SB_TILE
                k_all_b, v_all_b = _cache_join(cache_sb_k, cache_sb_v, ks_b, vs_b)
            xp, xpb = _layer_b(xp, xpb, lw, shared, mk_pb, mv_pb, kp_b, vp_b, batch=bp, seq=lp, seq_k=lp,
                               q_off=0)
            xs, xsb = _layer_b(xs, xsb, lw, shared, cmk[l], cmv[l], k_all_b, v_all_b, batch=bs, seq=ls,
                               seq_k=lk_pad, q_off=PAST_LEN)

    return (xp.reshape(bp, lp, D_MODEL), xs.reshape(bs, ls, D_MODEL),
            jnp.stack(gla_p, 0), jnp.stack(gla_s, 0), kp_f, vp_f, ks_f, vs_f,
            jnp.stack(mk_list, 0), jnp.stack(mv_list, 0))
```

```python
import functools
import math

import numpy as np
import jax
import jax.numpy as jnp
from jax import lax
from jax.experimental import pallas as pl
from jax.experimental.pallas import tpu as pltpu

D_MODEL = 2048
DEPTH = 4
PAST_LEN = 2048
CHUNK = 64
N_A = DEPTH // 2
HEAD_DIM = 128
MEM_HEADS = 4
MEM_WIDTH = MEM_HEADS * HEAD_DIM
MAIN_WIDTH = D_MODEL - MEM_WIDTH
GLA_HEADS = 4
GLA_DV = MAIN_WIDTH // GLA_HEADS
GLA_DK = GLA_DV // 2
GLA_KW = GLA_HEADS * GLA_DK
GLA_LOWRANK = 16
GLA_TAU = 16.0
SB_HEADS = MAIN_WIDTH // HEAD_DIM
D_FF = ((8 * D_MODEL // 3 + 255) // 256) * 256
DN_ALPHA = (2 * DEPTH) ** 0.25
LN_EPS = 1e-5
RMS_EPS = 1e-6
SB_SCALE = HEAD_DIM ** -0.5
MEM_SCALE = HEAD_DIM ** -0.5

V7X_LANES = 128
V7X_MXU_DIM = 256
V7X_VMEM_BYTES = 64 * 1024 * 1024

GLA_DKP = V7X_MXU_DIM
GLA_LRP = V7X_LANES
A_OFF_V = 0
A_OFF_G = MAIN_WIDTH
A_OFF_Q = 2 * MAIN_WIDTH
A_OFF_K = A_OFF_Q + GLA_HEADS * GLA_DKP
A_OFF_QM = A_OFF_K + GLA_HEADS * GLA_DKP
A_OFF_LR = A_OFF_QM + MEM_WIDTH
A_WIDTH = A_OFF_LR + GLA_LRP

SB_TILE = 256
SB_HEADS_PER_STEP = 4
SB_DEAD_LOG2 = 150.0
LOG2_E = math.log2(math.e)

F32 = jnp.float32
BF16 = jnp.bfloat16


def _vmem_limit(block_bytes, scratch_bytes=0, temp_bytes=0):
    need = 2 * block_bytes + scratch_bytes + temp_bytes + (2 << 20)
    return int(min(max(need, 16 << 20), V7X_VMEM_BYTES - (8 << 20)))


def _nbytes(shape, dtype):
    return int(np.prod(shape)) * jnp.dtype(dtype).itemsize


def _tile(n, pref):
    if n <= pref:
        return n
    t = pref
    while n % t:
        t -= 1
    return t


def _dot(a, b):
    return jnp.dot(a, b, preferred_element_type=F32)


def _dot_nt(a, b):
    return lax.dot_general(a, b, (((1,), (1,)), ((), ())), preferred_element_type=F32)


def _dot_tn(a, b):
    return lax.dot_general(a, b, (((0,), (0,)), ((), ())), preferred_element_type=F32)


def _sigmoid(x):
    return 1.0 / (1.0 + jnp.exp(-x))


def _softplus(x):
    return jnp.maximum(x, 0.0) + jnp.log(1.0 + jnp.exp(-jnp.abs(x)))


def _mm_kernel(*refs, has_scale):
    if has_scale:
        a_ref, w_ref, s_ref = refs[:3]
        o_refs = refs[3:]
    else:
        a_ref, w_ref = refs[:2]
        o_refs = refs[2:]
    acc = _dot(a_ref[...].astype(BF16), w_ref[...].astype(BF16))
    if has_scale:
        acc = acc * s_ref[...]
    for o_ref in o_refs:
        o_ref[...] = acc.astype(o_ref.dtype)


def _weight_spec(w, block, index_map, layer, **kwargs):
    if w.ndim == 2:
        return pl.BlockSpec(block, index_map, **kwargs)
    return pl.BlockSpec((None,) + block, lambda *g: (layer,) + index_map(*g), **kwargs)


def _matmul(a, w, *, out_dtypes, layer=None, col_scale=None, tm=1024, tn=1024):
    m, k = a.shape
    n = w.shape[-1]
    tm, tn = _tile(m, tm), _tile(n, tn)
    in_specs = [pl.BlockSpec((tm, k), lambda j, i: (i, 0)),
                _weight_spec(w, (k, tn), lambda j, i: (0, j), layer)]
    args = [a, w]
    if col_scale is not None:
        in_specs.append(pl.BlockSpec((1, tn), lambda j, i: (0, j)))
        args.append(col_scale)
    blocks = _nbytes((tm, k), a.dtype) + _nbytes((k, tn), w.dtype) + sum(_nbytes((tm, tn), d) for d in out_dtypes)
    outs = pl.pallas_call(
        functools.partial(_mm_kernel, has_scale=col_scale is not None),
        grid=(n // tn, m // tm),
        in_specs=in_specs,
        out_specs=[pl.BlockSpec((tm, tn), lambda j, i: (i, j)) for _ in out_dtypes],
        out_shape=[jax.ShapeDtypeStruct((m, n), d) for d in out_dtypes],
        compiler_params=pltpu.CompilerParams(
            dimension_semantics=("arbitrary", "arbitrary"),
            vmem_limit_bytes=_vmem_limit(blocks, temp_bytes=2 * _nbytes((tm, tn), F32))),
        name="proj_matmul",
    )(*args)
    return outs


def _swiglu_kernel(a_ref, wg_ref, wu_ref, o_ref, *, nsplit):
    rows_per = a_ref.shape[0] // nsplit
    wg = wg_ref[...].astype(BF16)
    wu = wu_ref[...].astype(BF16)
    for r in range(nsplit):
        rows = slice(r * rows_per, (r + 1) * rows_per)
        a = a_ref[rows, :]
        gate = _dot(a, wg)
        up = _dot(a, wu)
        o_ref[rows, :] = (gate * _sigmoid(gate) * up).astype(o_ref.dtype)


def _ffn_in(a, w, layer, *, tm=2048, tn=512):
    m, k = a.shape
    tm, tn = _tile(m, tm), _tile(D_FF, tn)
    nj = D_FF // tn
    blocks = _nbytes((tm, k), BF16) + 2 * _nbytes((k, tn), w.dtype) + _nbytes((tm, tn), BF16)
    return pl.pallas_call(
        functools.partial(_swiglu_kernel, nsplit=4 if tm % 1024 == 0 else 1),
        grid=(nj, m // tm),
        in_specs=[pl.BlockSpec((tm, k), lambda j, i: (i, 0)),
                  _weight_spec(w, (k, tn), lambda j, i: (0, j), layer),
                  _weight_spec(w, (k, tn), lambda j, i: (0, j + nj), layer)],
        out_specs=pl.BlockSpec((tm, tn), lambda j, i: (i, j)),
        out_shape=jax.ShapeDtypeStruct((m, D_FF), BF16),
        compiler_params=pltpu.CompilerParams(
            dimension_semantics=("arbitrary", "arbitrary"),
            vmem_limit_bytes=_vmem_limit(blocks, temp_bytes=4 * _nbytes((tm, tn), F32))),
        name="ffn_in_swiglu",
    )(a, w, w)


def _ln_kernel(*refs, nparts, nsplit):
    a_refs = refs[:nparts]
    w_ref, x_ref, g_ref, b_ref, of_ref, ob_ref = refs[nparts:]
    rows_per = x_ref.shape[0] // nsplit
    for r in range(nsplit):
        rows = slice(r * rows_per, (r + 1) * rows_per)
        a = a_refs[0][rows, :] if nparts == 1 else jnp.concatenate([p[rows, :] for p in a_refs], axis=1)
        y = DN_ALPHA * x_ref[rows, :] + _dot(a, w_ref[...])
        mu = jnp.mean(y, axis=-1, keepdims=True)
        d = y - mu
        var = jnp.mean(d * d, axis=-1, keepdims=True)
        out = d * lax.rsqrt(var + LN_EPS) * g_ref[...] + b_ref[...]
        of_ref[rows, :] = out
        ob_ref[rows, :] = out.astype(ob_ref.dtype)


def _matmul_residual_ln(a_parts, w, layer, x, gamma, beta, *, tm, nsplit=1):
    m, n = x.shape
    k = w.shape[-2]
    assert sum(a.shape[1] for a in a_parts) == k
    tm = _tile(m, tm)
    assert tm % nsplit == 0
    in_specs = [pl.BlockSpec((tm, a.shape[1]), lambda i: (i, 0)) for a in a_parts]
    in_specs += [_weight_spec(w, (k, n), lambda i: (0, 0), layer, pipeline_mode=pl.Buffered(1)),
                 pl.BlockSpec((tm, n), lambda i: (i, 0)),
                 pl.BlockSpec((1, n), lambda i: (0, 0)),
                 pl.BlockSpec((1, n), lambda i: (0, 0))]
    args = list(a_parts) + [w]
    resident = _nbytes((k, n), BF16)
    blocks = _nbytes((tm, k), BF16) + 2 * _nbytes((tm, n), F32) + _nbytes((tm, n), BF16)
    return pl.pallas_call(
        functools.partial(_ln_kernel, nparts=len(a_parts), nsplit=nsplit),
        grid=(m // tm,),
        in_specs=in_specs,
        out_specs=[pl.BlockSpec((tm, n), lambda i: (i, 0)),
                   pl.BlockSpec((tm, n), lambda i: (i, 0))],
        out_shape=[jax.ShapeDtypeStruct((m, n), F32), jax.ShapeDtypeStruct((m, n), BF16)],
        compiler_params=pltpu.CompilerParams(
            dimension_semantics=("arbitrary",),
            vmem_limit_bytes=_vmem_limit(blocks, scratch_bytes=resident,
                                         temp_bytes=3 * _nbytes((tm, n), F32))),
        name="matmul_residual_ln",
    )(*args, x, gamma, beta)


def _kv_kernel(a_ref, w_ref, of_ref, ob_ref):
    acc = _dot(a_ref[...], w_ref[...].astype(BF16))
    ob_ref[...] = acc.astype(ob_ref.dtype)
    for h in range(SB_HEADS):
        of_ref[0, :, h, :] = acc[:, h * HEAD_DIM:(h + 1) * HEAD_DIM]


def _kv_dma_kernel(a_ref, w_ref, of_hbm, ob_ref, stage_ref, sem_ref, *, nb, nsteps):
    i = pl.program_id(0)
    tm = a_ref.shape[0]
    slot = i % 2

    def head_copy(step, sl, h):
        rows = pl.ds((step % nb) * tm, tm)
        return pltpu.make_async_copy(stage_ref.at[sl, :, pl.ds(h * HEAD_DIM, HEAD_DIM)],
                                     of_hbm.at[step // nb, rows, h, :], sem_ref.at[sl])

    def wait_step(step, sl):
        for h in range(SB_HEADS):
            head_copy(step, sl, h).wait()

    @pl.when(i >= 2)
    def _():
        wait_step(i - 2, slot)

    acc = _dot(a_ref[...], w_ref[...].astype(BF16))
    ob_ref[...] = acc.astype(ob_ref.dtype)
    stage_ref[slot] = acc
    for h in range(SB_HEADS):
        head_copy(i, slot, h).start()

    @pl.when(i == nsteps - 1)
    def _():
        if nsteps >= 2:
            wait_step(i - 1, 1 - slot)
        wait_step(i, slot)


def _kv_proj_dma(a, w, col_blk, *, batch, seq, tm=512):
    m, k = a.shape
    tm = _tile(seq, tm)
    nb = seq // tm
    nsteps = m // tm
    blocks = _nbytes((tm, k), BF16) + _nbytes((tm, MAIN_WIDTH), BF16)
    return pl.pallas_call(
        functools.partial(_kv_dma_kernel, nb=nb, nsteps=nsteps),
        grid=(nsteps,),
        in_specs=[pl.BlockSpec((tm, k), lambda i: (i, 0)),
                  pl.BlockSpec((k, MAIN_WIDTH), lambda i: (0, col_blk), pipeline_mode=pl.Buffered(1))],
        out_specs=[pl.BlockSpec(memory_space=pl.ANY),
                   pl.BlockSpec((tm, MAIN_WIDTH), lambda i: (i, 0))],
        out_shape=[jax.ShapeDtypeStruct((batch, seq, SB_HEADS, HEAD_DIM), F32),
                   jax.ShapeDtypeStruct((m, MAIN_WIDTH), BF16)],
        scratch_shapes=[pltpu.VMEM((2, tm, MAIN_WIDTH), F32), pltpu.SemaphoreType.DMA((2,))],
        compiler_params=pltpu.CompilerParams(
            dimension_semantics=("arbitrary",),
            vmem_limit_bytes=_vmem_limit(
                blocks, scratch_bytes=_nbytes((k, MAIN_WIDTH), w.dtype) + _nbytes((2, tm, MAIN_WIDTH), F32),
                temp_bytes=2 * _nbytes((tm, MAIN_WIDTH), F32))),
        name="kv_proj_dma",
    )(a, w)


def _kv_proj(a, w, col_blk, *, batch, seq, tm=512):
    m, k = a.shape
    tm = _tile(seq, tm)
    nb = seq // tm
    sublane_padded_heads = -(-SB_HEADS // 8) * 8
    blocks = (_nbytes((tm, k), BF16) + _nbytes((tm, MAIN_WIDTH), BF16)
              + _nbytes((tm, sublane_padded_heads, HEAD_DIM), F32))
    return pl.pallas_call(
        _kv_kernel,
        grid=(m // tm,),
        in_specs=[pl.BlockSpec((tm, k), lambda i: (i, 0)),
                  pl.BlockSpec((k, MAIN_WIDTH), lambda i: (0, col_blk), pipeline_mode=pl.Buffered(1))],
        out_specs=[pl.BlockSpec((1, tm, SB_HEADS, HEAD_DIM), lambda i: (i // nb, i % nb, 0, 0)),
                   pl.BlockSpec((tm, MAIN_WIDTH), lambda i: (i, 0))],
        out_shape=[jax.ShapeDtypeStruct((batch, seq, SB_HEADS, HEAD_DIM), F32),
                   jax.ShapeDtypeStruct((m, MAIN_WIDTH), BF16)],
        compiler_params=pltpu.CompilerParams(
            dimension_semantics=("arbitrary",),
            vmem_limit_bytes=_vmem_limit(blocks, scratch_bytes=_nbytes((k, MAIN_WIDTH), w.dtype),
                                         temp_bytes=2 * _nbytes((tm, MAIN_WIDTH), F32))),
        name="kv_proj",
    )(a, w)


def _gla_tables(c):
    nlev = int(math.log2(c))
    assert 1 << nlev == c
    idx = np.arange(c)
    i, t = idx[:, None], idx[None, :]
    sums, pairs = [], []
    for lev in range(nlev):
        h = c >> (lev + 1)
        p = (i // (2 * h)) * 2 * h + h - 1
        second = (i % (2 * h)) >= h
        sums.append(np.where(second, (t > p) & (t <= i), (t > i) & (t <= p)))
        pairs.append((i // (2 * h) == t // (2 * h)) & second & ((t % (2 * h)) < h))
    sums.append(t <= i)
    sums.append(t > i)
    pairs.append(i == t)
    return (np.concatenate(sums, 0).astype(np.float32), np.stack(pairs, 0).astype(np.float32), nlev)


def _gla_kernel(v_ref, g_ref, q_ref, k_ref, lr_ref, wlr_ref, bg_ref, gn_ref, sum_ref, pair_ref, s0_ref,
                o_ref, sf_ref, st_ref, la_ref, og_ref, *, c, nchunk, nlev, nt):
    t = pl.program_id(1)

    @pl.when(t == 0)
    def _():
        pad = jnp.zeros((GLA_DKP - GLA_DK, GLA_DV), F32)
        for h in range(GLA_HEADS):
            st_ref[h] = jnp.concatenate([s0_ref[0, h], pad], axis=0).T

    zg = _dot(lr_ref[...], wlr_ref[...]) + bg_ref[...]
    la_ref[...] = (-_softplus(-zg) * (1.0 / GLA_TAU)).astype(BF16)

    def head(h, rows):
        kc = slice(h * GLA_DKP, (h + 1) * GLA_DKP)
        vc = slice(h * GLA_DV, (h + 1) * GLA_DV)
        w = jnp.exp(_dot(sum_ref[...], la_ref[rows, kc]))
        qb = q_ref[rows, kc]
        kb = k_ref[rows, kc]
        vb = v_ref[rows, vc]
        q = qb.astype(F32)
        k = kb.astype(F32)
        att = pair_ref[nlev] * _dot_nt(qb, kb)
        for lev in range(nlev):
            wl = w[lev * c:(lev + 1) * c]
            att = att + pair_ref[lev] * _dot_nt((q * wl).astype(BF16), (k * wl).astype(BF16))
        w_pre = w[nlev * c:(nlev + 1) * c]
        w_suf = w[(nlev + 1) * c:(nlev + 2) * c]
        st = st_ref[h]
        og_ref[rows, vc] = _dot(att.astype(BF16), vb) + _dot_nt((q * w_pre).astype(BF16), st.astype(BF16))
        st_ref[h] = st * w_pre[c - 1:c, :] + _dot_tn(vb, (k * w_suf).astype(BF16))

    def chunk(ci, carry):
        rows = pl.ds(pl.multiple_of(ci * c, c), c)
        for h in range(GLA_HEADS):
            head(h, rows)
        return carry

    lax.fori_loop(0, nchunk, chunk, 0, unroll=min(2, nchunk))

    for h in range(GLA_HEADS):
        vc = slice(h * GLA_DV, (h + 1) * GLA_DV)
        o = og_ref[:, vc]
        ms = jnp.mean(o * o, axis=-1, keepdims=True)
        on = o * lax.rsqrt(ms + RMS_EPS) * gn_ref[...]
        g = g_ref[:, vc].astype(F32)
        o_ref[:, vc] = (on * (g * _sigmoid(g))).astype(o_ref.dtype)

    @pl.when(t == nt - 1)
    def _():
        for h in range(GLA_HEADS):
            sf_ref[0, h] = st_ref[h].T[:GLA_DK, :]


def _gla(proj, w_lr2p, b_gatep, g_norm, s0, *, batch, seq):
    c = CHUNK if seq % CHUNK == 0 else seq
    tt = _tile(seq, 512)
    if tt % c:
        tt = c
    nt = seq // tt
    sums_np, pairs_np, nlev = _gla_tables(c)
    sums = jnp.asarray(sums_np, BF16)
    pairs = jnp.asarray(pairs_np, F32)
    nrow = sums_np.shape[0]
    kw = GLA_HEADS * GLA_DKP
    state_blk = (1, GLA_HEADS, GLA_DK, GLA_DV)
    state_vmem = (GLA_HEADS, GLA_DV, GLA_DKP)

    in_specs = [
        pl.BlockSpec((tt, MAIN_WIDTH), lambda b, t: (b * nt + t, A_OFF_V // MAIN_WIDTH)),
        pl.BlockSpec((tt, MAIN_WIDTH), lambda b, t: (b * nt + t, A_OFF_G // MAIN_WIDTH)),
        pl.BlockSpec((tt, kw), lambda b, t: (b * nt + t, A_OFF_Q // kw)),
        pl.BlockSpec((tt, kw), lambda b, t: (b * nt + t, A_OFF_K // kw)),
        pl.BlockSpec((tt, GLA_LRP), lambda b, t: (b * nt + t, A_OFF_LR // GLA_LRP)),
        pl.BlockSpec((GLA_LRP, kw), lambda b, t: (0, 0)),
        pl.BlockSpec((1, kw), lambda b, t: (0, 0)),
        pl.BlockSpec((1, GLA_DV), lambda b, t: (0, 0)),
        pl.BlockSpec((nrow, c), lambda b, t: (0, 0)),
        pl.BlockSpec((nlev + 1, c, c), lambda b, t: (0, 0, 0)),
        pl.BlockSpec(state_blk, lambda b, t: (b, 0, 0, 0)),
    ]
    blocks = (3 * _nbytes((tt, MAIN_WIDTH), BF16) + 2 * _nbytes((tt, kw), BF16) + _nbytes((tt, GLA_LRP), BF16)
              + _nbytes((GLA_LRP, kw), BF16) + 2 * _nbytes(state_blk, F32) + _nbytes((nrow, c), BF16)
              + _nbytes((nlev + 1, c, c), F32))
    return pl.pallas_call(
        functools.partial(_gla_kernel, c=c, nchunk=tt // c, nlev=nlev, nt=nt),
        grid=(batch, nt),
        in_specs=in_specs,
        out_specs=[pl.BlockSpec((tt, MAIN_WIDTH), lambda b, t: (b * nt + t, 0)),
                   pl.BlockSpec(state_blk, lambda b, t: (b, 0, 0, 0))],
        out_shape=[jax.ShapeDtypeStruct((batch * seq, MAIN_WIDTH), BF16),
                   jax.ShapeDtypeStruct((batch, GLA_HEADS, GLA_DK, GLA_DV), F32)],
        scratch_shapes=[pltpu.VMEM(state_vmem, F32), pltpu.VMEM((tt, kw), BF16),
                        pltpu.VMEM((tt, MAIN_WIDTH), F32)],
        compiler_params=pltpu.CompilerParams(
            dimension_semantics=("arbitrary", "arbitrary"),
            vmem_limit_bytes=_vmem_limit(
                blocks, temp_bytes=8 << 20,
                scratch_bytes=_nbytes(state_vmem, F32) + _nbytes((tt, kw), BF16) + _nbytes((tt, MAIN_WIDTH), F32))),
        name="gla_scan",
    )(proj, proj, proj, proj, proj, w_lr2p, b_gatep, g_norm, sums, pairs, s0)


def _sb_kernel(q_ref, k_ref, v_ref, u_ref, o_ref, acc_ref, carry_ref, *, tq, tk, q_off, nh):
    qi = pl.program_id(2)
    jd = (q_off + qi * tq) // tk
    q_pos = q_off + qi * tq + lax.broadcasted_iota(jnp.int32, (tq, tk), 0)
    ntile = tk // V7X_LANES

    def scores(h, j, reach):
        cols = slice(h * HEAD_DIM, (h + 1) * HEAD_DIM)
        z = _dot_nt(q_ref[:, cols], k_ref[pl.ds(pl.multiple_of(j * tk, tk), tk), cols])
        sp = jnp.maximum(z, 0.0) + jnp.log2(1.0 + jnp.exp2(-jnp.abs(z)))
        if reach is not None:
            sp = jnp.where(reach, sp, 0.0)
        return z, sp

    def suffix_sums(sps):
        stacked = jnp.concatenate([sp.astype(BF16) for sp in sps], axis=0)
        cs = _dot(stacked, u_ref[...])
        return [cs[n * tq:(n + 1) * tq] for n in range(len(sps))]

    def weighted(h, j, z, cs, carry, reach):
        cols = slice(h * HEAD_DIM, (h + 1) * HEAD_DIM)
        w = jnp.exp2(z - cs - jnp.concatenate([carry] * ntile, axis=1))
        if reach is not None:
            w = jnp.where(reach, w, 0.0)
        return _dot(w.astype(BF16), v_ref[pl.ds(pl.multiple_of(j * tk, tk), tk), cols])

    def live(carries):
        low = functools.reduce(jnp.minimum, carries)
        return jnp.min(low, axis=0, keepdims=True)[0, 0] < SB_DEAD_LOG2

    k_col = lax.broadcasted_iota(jnp.int32, (tq, tk), 1)
    diag_reach = (jd * tk + k_col) < q_pos
    has_prev = jnp.where(jd >= 1, 1.0, 0.0)
    j_prev = jnp.maximum(jd - 1, 0)
    zs, sps = [], []
    for h in range(nh):
        for j, reach in ((jd, diag_reach), (j_prev, None)):
            z, sp = scores(h, j, reach)
            zs.append(z)
            sps.append(sp)
    css = suffix_sums(sps)
    carries = []
    for h in range(nh):
        c0 = jnp.broadcast_to(jnp.sum(sps[2 * h], axis=1, keepdims=True), (tq, V7X_LANES))
        c1 = c0 + jnp.sum(sps[2 * h + 1], axis=1, keepdims=True) * has_prev
        a0 = weighted(h, jd, zs[2 * h], css[2 * h], jnp.zeros((tq, V7X_LANES), F32), diag_reach)
        a1 = weighted(h, j_prev, zs[2 * h + 1], css[2 * h + 1], c0, None)
        acc_ref[h] = a0 + a1 * has_prev
        carry_ref[h] = c1
        carries.append(c1)

    def cond(state):
        j, alive = state
        return jnp.logical_and(j >= 0, alive)

    def body(state):
        j, _ = state
        zs, sps = zip(*[scores(h, j, None) for h in range(nh)])
        css = suffix_sums(sps)
        new = []
        for h in range(nh):
            carry = carry_ref[h]
            acc_ref[h] += weighted(h, j, zs[h], css[h], carry, None)
            carry_ref[h] = carry + jnp.sum(sps[h], axis=1, keepdims=True)
            new.append(carry_ref[h])
        return j - 1, live(new)

    lax.while_loop(cond, body, (jd - 2, live(carries)))
    for h in range(nh):
        o_ref[:, h * HEAD_DIM:(h + 1) * HEAD_DIM] = acc_ref[h].astype(o_ref.dtype)


def _sb_attention(q_arr, q_col0, k_arr, v_arr, *, batch, seq_q, seq_k, q_off):
    tq = min(SB_TILE, seq_q)
    tk = SB_TILE
    nh = SB_HEADS_PER_STEP
    width = nh * HEAD_DIM
    assert seq_q % tq == 0 and seq_k % tk == 0 and q_off % tk == 0 and tq <= tk
    assert q_col0 % width == 0 and SB_HEADS % nh == 0
    nq = seq_q // tq
    u = jnp.asarray(np.tril(np.ones((tk, tk), np.float32)), BF16)
    blocks = 2 * _nbytes((tq, width), BF16) + 2 * _nbytes((seq_k, width), BF16) + _nbytes((tk, tk), BF16)
    return pl.pallas_call(
        functools.partial(_sb_kernel, tq=tq, tk=tk, q_off=q_off, nh=nh),
        grid=(batch, SB_HEADS // nh, nq),
        in_specs=[pl.BlockSpec((tq, width), lambda b, h, i: (b * nq + i, q_col0 // width + h)),
                  pl.BlockSpec((seq_k, width), lambda b, h, i: (b, h)),
                  pl.BlockSpec((seq_k, width), lambda b, h, i: (b, h)),
                  pl.BlockSpec((tk, tk), lambda b, h, i: (0, 0))],
        out_specs=pl.BlockSpec((tq, width), lambda b, h, i: (b * nq + i, h)),
        out_shape=jax.ShapeDtypeStruct((batch * seq_q, MAIN_WIDTH), BF16),
        scratch_shapes=[pltpu.VMEM((nh, tq, HEAD_DIM), F32), pltpu.VMEM((nh, tq, V7X_LANES), F32)],
        compiler_params=pltpu.CompilerParams(
            dimension_semantics=("arbitrary", "arbitrary", "arbitrary"),
            vmem_limit_bytes=_vmem_limit(blocks, temp_bytes=12 * nh * _nbytes((tq, tk), F32))),
        name="stick_breaking",
    )(q_arr, k_arr, v_arr, u)


def _cache_join_kernel(ck_ref, cv_ref, nk_ref, nv_ref, ok_ref, ov_ref, *, past):
    for c_ref, n_ref, o_ref in ((ck_ref, nk_ref, ok_ref), (cv_ref, nv_ref, ov_ref)):
        o_ref[0:past, :] = c_ref[0, 0].astype(o_ref.dtype)
        zeros = jnp.zeros((o_ref.shape[0] - past - n_ref.shape[0], o_ref.shape[1]), o_ref.dtype)
        o_ref[past:, :] = jnp.concatenate([n_ref[...], zeros], axis=0)


def _cache_join(cache_k, cache_v, new_k, new_v):
    bs, past, nheads, hd = cache_k.shape
    new_len = new_k.shape[0] // bs
    lk_pad = past + SB_TILE
    assert past % SB_TILE == 0 and new_len <= SB_TILE and new_len % 16 == 0
    cache_spec = pl.BlockSpec((1, 1, past, hd), lambda b, h: (b, h, 0, 0))
    new_spec = pl.BlockSpec((new_len, hd), lambda b, h: (b, h))
    out_spec = pl.BlockSpec((lk_pad, hd), lambda b, h: (b, h))
    out_sds = jax.ShapeDtypeStruct((bs * lk_pad, nheads * hd), BF16)
    blocks = 2 * (_nbytes((past, hd), F32) + _nbytes((new_len, hd), BF16) + _nbytes((lk_pad, hd), BF16))
    return pl.pallas_call(
        functools.partial(_cache_join_kernel, past=past),
        grid=(bs, nheads),
        in_specs=[cache_spec, cache_spec, new_spec, new_spec],
        out_specs=[out_spec, out_spec],
        out_shape=[out_sds, out_sds],
        compiler_params=pltpu.CompilerParams(
            dimension_semantics=("arbitrary", "arbitrary"),
            vmem_limit_bytes=_vmem_limit(blocks)),
        name="cache_join",
    )(jnp.swapaxes(cache_k, 1, 2), jnp.swapaxes(cache_v, 1, 2), new_k, new_v)


def _mem_kernel(qm_ref, mk_ref, mv_ref, o_ref):
    for h in range(MEM_HEADS):
        cols = slice(h * HEAD_DIM, (h + 1) * HEAD_DIM)
        s = _dot_nt(qm_ref[:, cols], mk_ref[0, :, cols])
        e = jnp.exp(s - jnp.max(s, axis=-1, keepdims=True))
        den = jnp.sum(e, axis=-1, keepdims=True)
        o = _dot(e.astype(BF16), mv_ref[0, :, cols]) / den
        o_ref[:, cols] = o.astype(o_ref.dtype)


def _mem_attention(q_arr, q_col0, mk, mv, *, batch, seq):
    tt = _tile(seq, 512)
    nt = seq // tt
    mlen = mk.shape[1]
    assert q_col0 % MEM_WIDTH == 0
    blocks = 2 * _nbytes((tt, MEM_WIDTH), BF16) + 2 * _nbytes((mlen, MEM_WIDTH), BF16)
    return pl.pallas_call(
        _mem_kernel,
        grid=(batch, nt),
        in_specs=[pl.BlockSpec((tt, MEM_WIDTH), lambda b, t: (b * nt + t, q_col0 // MEM_WIDTH)),
                  pl.BlockSpec((1, mlen, MEM_WIDTH), lambda b, t: (b, 0, 0)),
                  pl.BlockSpec((1, mlen, MEM_WIDTH), lambda b, t: (b, 0, 0))],
        out_specs=pl.BlockSpec((tt, MEM_WIDTH), lambda b, t: (b * nt + t, 0)),
        out_shape=jax.ShapeDtypeStruct((batch * seq, MEM_WIDTH), BF16),
        compiler_params=pltpu.CompilerParams(
            dimension_semantics=("arbitrary", "arbitrary"),
            vmem_limit_bytes=_vmem_limit(blocks, temp_bytes=6 * _nbytes((tt, mlen), F32))),
        name="mem_attention",
    )(q_arr, mk, mv)


def _pad_heads(w, width, padded):
    lead = w.shape[:-1]
    w = w.reshape(lead + (GLA_HEADS, width))
    w = jnp.pad(w, [(0, 0)] * len(lead) + [(0, 0), (0, padded - width)])
    return w.reshape(lead + (GLA_HEADS * padded,))


def _layout_in_a(w_in):
    q = w_in[:, :GLA_KW]
    k = w_in[:, GLA_KW:2 * GLA_KW]
    v = w_in[:, 2 * GLA_KW:2 * GLA_KW + MAIN_WIDTH]
    g = w_in[:, 2 * GLA_KW + MAIN_WIDTH:2 * GLA_KW + 2 * MAIN_WIDTH]
    lr = w_in[:, 2 * GLA_KW + 2 * MAIN_WIDTH:2 * GLA_KW + 2 * MAIN_WIDTH + GLA_LOWRANK]
    qm = w_in[:, 2 * GLA_KW + 2 * MAIN_WIDTH + GLA_LOWRANK:]
    lr = jnp.pad(lr, ((0, 0), (0, GLA_LRP - GLA_LOWRANK)))
    w = jnp.concatenate([v, g, _pad_heads(q, GLA_DK, GLA_DKP), _pad_heads(k, GLA_DK, GLA_DKP), qm, lr], axis=1)
    scale = np.ones((1, A_WIDTH), np.float32)
    scale[:, A_OFF_Q:A_OFF_K] = GLA_DK ** -0.5
    scale[:, A_OFF_QM:A_OFF_LR] = MEM_SCALE
    return w.astype(BF16), jnp.asarray(scale)


def _scale_in_b():
    scale = np.ones((1, D_MODEL), np.float32)
    scale[:, :MAIN_WIDTH] = SB_SCALE * LOG2_E
    scale[:, MAIN_WIDTH:] = MEM_SCALE
    return jnp.asarray(scale)


def _finish_layer(x, o_main, o_mem, lw, shared):
    l = lw["layer"]
    x1, x1b = _matmul_residual_ln([o_main, o_mem], shared["w_o"], l, x, lw["ln1_g"], lw["ln1_b"], tm=512,
                                  nsplit=4)
    hid = _ffn_in(x1b, shared["w_ffn_in"], l)
    return _matmul_residual_ln([hid], shared["w_ffn_out"], l, x1, lw["ln2_g"], lw["ln2_b"], tm=256)


def _layer_a(x, xb, lw, shared, mk, mv, s0, *, batch, seq):
    proj, = _matmul(xb, lw["w_in"], out_dtypes=(BF16,), col_scale=lw["in_scale"], tm=1024, tn=1920)
    o_main, s_final = _gla(proj, lw["w_lr2p"], lw["b_gatep"], lw["g_norm"], s0, batch=batch, seq=seq)
    o_mem = _mem_attention(proj, A_OFF_QM, mk, mv, batch=batch, seq=seq)
    x, xb = _finish_layer(x, o_main, o_mem, lw, shared)
    return x, xb, s_final


def _layer_b(x, xb, lw, shared, mk, mv, kb, vb, *, batch, seq, seq_k, q_off):
    proj, = _matmul(xb, shared["w_in_b"], layer=lw["layer"] - N_A, out_dtypes=(BF16,),
                    col_scale=shared["in_b_scale"], tm=1024, tn=1024)
    o_main = _sb_attention(proj, 0, kb, vb, batch=batch, seq_q=seq, seq_k=seq_k, q_off=q_off)
    o_mem = _mem_attention(proj, MAIN_WIDTH, mk, mv, batch=batch, seq=seq)
    return _finish_layer(x, o_main, o_mem, lw, shared)


def kernel(x_prompt, x_sample, mem_prompt, state_gla, cache_sb_k, cache_sb_v, cache_mem_k, cache_mem_v,
           w_in_a, w_gate_lr, b_gate, gla_norm_g, w_in_b, w_kv_shared, w_mem_kv, w_o, ln1_g, ln1_b,
           ln2_g, ln2_b, w_ffn_in, w_ffn_out):
    bp, lp, _ = x_prompt.shape
    bs, ls, _ = x_sample.shape
    mlen = mem_prompt.shape[1]

    shared = {
        "w_o": w_o.astype(BF16),
        "w_ffn_in": w_ffn_in,
        "w_ffn_out": w_ffn_out.astype(BF16),
        "w_in_b": w_in_b,
        "in_b_scale": _scale_in_b(),
    }
    layers = []
    for l in range(DEPTH):
        lw = {
            "layer": l,
            "ln1_g": ln1_g[l][None, :], "ln1_b": ln1_b[l][None, :],
            "ln2_g": ln2_g[l][None, :], "ln2_b": ln2_b[l][None, :],
        }
        if l < N_A:
            lw["w_in"], lw["in_scale"] = _layout_in_a(w_in_a[l])
            lr2 = _pad_heads(w_gate_lr[l], GLA_DK, GLA_DKP)
            lw["w_lr2p"] = jnp.pad(lr2, ((0, GLA_LRP - GLA_LOWRANK), (0, 0))).astype(BF16)
            lw["b_gatep"] = _pad_heads(b_gate[l][None, :], GLA_DK, GLA_DKP)
            lw["g_norm"] = gla_norm_g[l][None, :]
        layers.append(lw)
    w_kv = w_kv_shared
    w_memkv = w_mem_kv

    xp = x_prompt.reshape(bp * lp, D_MODEL)
    xs = x_sample.reshape(bs * ls, D_MODEL)
    xpb, xsb = xp, xs
    memb = mem_prompt.reshape(bp * mlen, D_MODEL).astype(BF16)
    cmk = cache_mem_k.reshape(DEPTH, bs, mlen, MEM_WIDTH).astype(BF16)
    cmv = cache_mem_v.reshape(DEPTH, bs, mlen, MEM_WIDTH).astype(BF16)

    s_zero = jnp.zeros((bp, GLA_HEADS, GLA_DK, GLA_DV), F32)
    gla_p, gla_s, mk_list, mv_list = [], [], [], []
    for l in range(DEPTH):
        lw = layers[l]
        mkv, = _matmul(memb, w_memkv, layer=l, out_dtypes=(F32,), tm=1024, tn=1024)
        mk_p, mv_p = mkv[:, :MEM_WIDTH], mkv[:, MEM_WIDTH:]
        mk_list.append(mk_p.reshape(bp, mlen, MEM_HEADS, HEAD_DIM))
        mv_list.append(mv_p.reshape(bp, mlen, MEM_HEADS, HEAD_DIM))
        mk_pb = mk_p.astype(BF16).reshape(bp, mlen, MEM_WIDTH)
        mv_pb = mv_p.astype(BF16).reshape(bp, mlen, MEM_WIDTH)
        if l < N_A:
            xp, xpb, sp = _layer_a(xp, xpb, lw, shared, mk_pb, mv_pb, s_zero, batch=bp, seq=lp)
            xs, xsb, ss = _layer_a(xs, xsb, lw, shared, cmk[l], cmv[l], state_gla[l], batch=bs, seq=ls)
            gla_p.append(sp)
            gla_s.append(ss)
        else:
            if l == N_A:
                kp_f, kp_b = _kv_proj_dma(xpb, w_kv, 0, batch=bp, seq=lp)
                vp_f, vp_b = _kv_proj_dma(xpb, w_kv, 1, batch=bp, seq=lp)
                ks_f, ks_b = _kv_proj(xsb, w_kv, 0, batch=bs, seq=ls)
                vs_f, vs_b = _kv_proj(xsb, w_kv, 1, batch=bs, seq=ls)
                lk_pad = PAST_LEN + SB_TILE
                k_all_b, v_all_b = _cache_join(cache_sb_k, cache_sb_v, ks_b, vs_b)
            xp, xpb = _layer_b(xp, xpb, lw, shared, mk_pb, mv_pb, kp_b, vp_b, batch=bp, seq=lp, seq_k=lp,
                               q_off=0)
            xs, xsb = _layer_b(xs, xsb, lw, shared, cmk[l], cmv[l], k_all_b, v_all_b, batch=bs, seq=ls,
                               seq_k=lk_pad, q_off=PAST_LEN)

    return (xp.reshape(bp, lp, D_MODEL), xs.reshape(bs, ls, D_MODEL),
            jnp.stack(gla_p, 0), jnp.stack(gla_s, 0), kp_f, vp_f, ks_f, vs_f,
            jnp.stack(mk_list, 0), jnp.stack(mv_list, 0))
```

```python
import functools
import math

import numpy as np
import jax
import jax.numpy as jnp
from jax import lax
from jax.experimental import pallas as pl
from jax.experimental.pallas import tpu as pltpu

D_MODEL = 2048
DEPTH = 4
PAST_LEN = 2048
CHUNK = 64
N_A = DEPTH // 2
HEAD_DIM = 128
MEM_HEADS = 4
MEM_WIDTH = MEM_HEADS * HEAD_DIM
MAIN_WIDTH = D_MODEL - MEM_WIDTH
GLA_HEADS = 4
GLA_DV = MAIN_WIDTH // GLA_HEADS
GLA_DK = GLA_DV // 2
GLA_KW = GLA_HEADS * GLA_DK
GLA_LOWRANK = 16
GLA_TAU = 16.0
SB_HEADS = MAIN_WIDTH // HEAD_DIM
D_FF = ((8 * D_MODEL // 3 + 255) // 256) * 256
DN_ALPHA = (2 * DEPTH) ** 0.25
LN_EPS = 1e-5
RMS_EPS = 1e-6
SB_SCALE = HEAD_DIM ** -0.5
MEM_SCALE = HEAD_DIM ** -0.5

V7X_LANES = 128
V7X_MXU_DIM = 256
V7X_VMEM_BYTES = 64 * 1024 * 1024

GLA_DKP = V7X_MXU_DIM
GLA_LRP = V7X_LANES
A_OFF_V = 0
A_OFF_G = MAIN_WIDTH
A_OFF_Q = 2 * MAIN_WIDTH
A_OFF_K = A_OFF_Q + GLA_HEADS * GLA_DKP
A_OFF_QM = A_OFF_K + GLA_HEADS * GLA_DKP
A_OFF_LR = A_OFF_QM + MEM_WIDTH
A_WIDTH = A_OFF_LR + GLA_LRP

SB_TILE = 256
SB_HEADS_PER_STEP = 4
SB_DEAD_LOG2 = 150.0
LOG2_E = math.log2(math.e)

F32 = jnp.float32
BF16 = jnp.bfloat16


def _vmem_limit(block_bytes, scratch_bytes=0, temp_bytes=0):
    need = 2 * block_bytes + scratch_bytes + temp_bytes + (2 << 20)
    return int(min(max(need, 16 << 20), V7X_VMEM_BYTES - (8 << 20)))


def _nbytes(shape, dtype):
    return int(np.prod(shape)) * jnp.dtype(dtype).itemsize


def _tile(n, pref):
    if n <= pref:
        return n
    t = pref
    while n % t:
        t -= 1
    return t


def _dot(a, b):
    return jnp.dot(a, b, preferred_element_type=F32)


def _dot_nt(a, b):
    return lax.dot_general(a, b, (((1,), (1,)), ((), ())), preferred_element_type=F32)


def _dot_tn(a, b):
    return lax.dot_general(a, b, (((0,), (0,)), ((), ())), preferred_element_type=F32)


def _sigmoid(x):
    return 1.0 / (1.0 + jnp.exp(-x))


def _softplus(x):
    return jnp.maximum(x, 0.0) + jnp.log(1.0 + jnp.exp(-jnp.abs(x)))


def _mm_kernel(*refs, has_scale):
    if has_scale:
        a_ref, w_ref, s_ref = refs[:3]
        o_refs = refs[3:]
    else:
        a_ref, w_ref = refs[:2]
        o_refs = refs[2:]
    acc = _dot(a_ref[...].astype(BF16), w_ref[...].astype(BF16))
    if has_scale:
        acc = acc * s_ref[...]
    for o_ref in o_refs:
        o_ref[...] = acc.astype(o_ref.dtype)


def _weight_spec(w, block, index_map, layer, **kwargs):
    if w.ndim == 2:
        return pl.BlockSpec(block, index_map, **kwargs)
    return pl.BlockSpec((None,) + block, lambda *g: (layer,) + index_map(*g), **kwargs)


def _matmul(a, w, *, out_dtypes, layer=None, col_scale=None, tm=1024, tn=1024):
    m, k = a.shape
    n = w.shape[-1]
    tm, tn = _tile(m, tm), _tile(n, tn)
    in_specs = [pl.BlockSpec((tm, k), lambda j, i: (i, 0)),
                _weight_spec(w, (k, tn), lambda j, i: (0, j), layer)]
    args = [a, w]
    if col_scale is not None:
        in_specs.append(pl.BlockSpec((1, tn), lambda j, i: (0, j)))
        args.append(col_scale)
    blocks = _nbytes((tm, k), a.dtype) + _nbytes((k, tn), w.dtype) + sum(_nbytes((tm, tn), d) for d in out_dtypes)
    outs = pl.pallas_call(
        functools.partial(_mm_kernel, has_scale=col_scale is not None),
        grid=(n // tn, m // tm),
        in_specs=in_specs,
        out_specs=[pl.BlockSpec((tm, tn), lambda j, i: (i, j)) for _ in out_dtypes],
        out_shape=[jax.ShapeDtypeStruct((m, n), d) for d in out_dtypes],
        compiler_params=pltpu.CompilerParams(
            dimension_semantics=("arbitrary", "arbitrary"),
            vmem_limit_bytes=_vmem_limit(blocks, temp_bytes=2 * _nbytes((tm, tn), F32))),
        name="proj_matmul",
    )(*args)
    return outs


def _swiglu_kernel(a_ref, wg_ref, wu_ref, o_ref, *, nsplit):
    rows_per = a_ref.shape[0] // nsplit
    wg = wg_ref[...].astype(BF16)
    wu = wu_ref[...].astype(BF16)
    for r in range(nsplit):
        rows = slice(r * rows_per, (r + 1) * rows_per)
        a = a_ref[rows, :]
        gate = _dot(a, wg)
        up = _dot(a, wu)
        o_ref[rows, :] = (gate * _sigmoid(gate) * up).astype(o_ref.dtype)


def _ffn_in(a, w, layer, *, tm=2048, tn=512):
    m, k = a.shape
    tm, tn = _tile(m, tm), _tile(D_FF, tn)
    nj = D_FF // tn
    blocks = _nbytes((tm, k), BF16) + 2 * _nbytes((k, tn), w.dtype) + _nbytes((tm, tn), BF16)
    return pl.pallas_call(
        functools.partial(_swiglu_kernel, nsplit=4 if tm % 1024 == 0 else 1),
        grid=(nj, m // tm),
        in_specs=[pl.BlockSpec((tm, k), lambda j, i: (i, 0)),
                  _weight_spec(w, (k, tn), lambda j, i: (0, j), layer),
                  _weight_spec(w, (k, tn), lambda j, i: (0, j + nj), layer)],
        out_specs=pl.BlockSpec((tm, tn), lambda j, i: (i, j)),
        out_shape=jax.ShapeDtypeStruct((m, D_FF), BF16),
        compiler_params=pltpu.CompilerParams(
            dimension_semantics=("arbitrary", "arbitrary"),
            vmem_limit_bytes=_vmem_limit(blocks, temp_bytes=4 * _nbytes((tm, tn), F32))),
        name="ffn_in_swiglu",
    )(a, w, w)


def _ln_kernel(*refs, nparts, nsplit):
    a_refs = refs[:nparts]
    w_ref, x_ref, g_ref, b_ref, of_ref, ob_ref = refs[nparts:]
    rows_per = x_ref.shape[0] // nsplit
    for r in range(nsplit):
        rows = slice(r * rows_per, (r + 1) * rows_per)
        a = a_refs[0][rows, :] if nparts == 1 else jnp.concatenate([p[rows, :] for p in a_refs], axis=1)
        y = DN_ALPHA * x_ref[rows, :] + _dot(a, w_ref[...])
        mu = jnp.mean(y, axis=-1, keepdims=True)
        d = y - mu
        var = jnp.mean(d * d, axis=-1, keepdims=True)
        out = d * lax.rsqrt(var + LN_EPS) * g_ref[...] + b_ref[...]
        of_ref[rows, :] = out
        ob_ref[rows, :] = out.astype(ob_ref.dtype)


def _matmul_residual_ln(a_parts, w, layer, x, gamma, beta, *, tm, nsplit=1):
    m, n = x.shape
    k = w.shape[-2]
    assert sum(a.shape[1] for a in a_parts) == k
    tm = _tile(m, tm)
    assert tm % nsplit == 0
    in_specs = [pl.BlockSpec((tm, a.shape[1]), lambda i: (i, 0)) for a in a_parts]
    in_specs += [_weight_spec(w, (k, n), lambda i: (0, 0), layer, pipeline_mode=pl.Buffered(1)),
                 pl.BlockSpec((tm, n), lambda i: (i, 0)),
                 pl.BlockSpec((1, n), lambda i: (0, 0)),
                 pl.BlockSpec((1, n), lambda i: (0, 0))]
    args = list(a_parts) + [w]
    resident = _nbytes((k, n), BF16)
    blocks = _nbytes((tm, k), BF16) + 2 * _nbytes((tm, n), F32) + _nbytes((tm, n), BF16)
    return pl.pallas_call(
        functools.partial(_ln_kernel, nparts=len(a_parts), nsplit=nsplit),
        grid=(m // tm,),
        in_specs=in_specs,
        out_specs=[pl.BlockSpec((tm, n), lambda i: (i, 0)),
                   pl.BlockSpec((tm, n), lambda i: (i, 0))],
        out_shape=[jax.ShapeDtypeStruct((m, n), F32), jax.ShapeDtypeStruct((m, n), BF16)],
        compiler_params=pltpu.CompilerParams(
            dimension_semantics=("arbitrary",),
            vmem_limit_bytes=_vmem_limit(blocks, scratch_bytes=resident,
                                         temp_bytes=3 * _nbytes((tm, n), F32))),
        name="matmul_residual_ln",
    )(*args, x, gamma, beta)


def _kv_kernel(a_ref, w_ref, of_ref, ob_ref):
    acc = _dot(a_ref[...], w_ref[...].astype(BF16))
    ob_ref[...] = acc.astype(ob_ref.dtype)
    for h in range(SB_HEADS):
        of_ref[0, :, h, :] = acc[:, h * HEAD_DIM:(h + 1) * HEAD_DIM]


def _kv_dma_kernel(a_ref, w_ref, of_hbm, ob_ref, stage_ref, sem_ref, *, nb, nsteps):
    i = pl.program_id(0)
    tm = a_ref.shape[0]
    slot = i % 2

    def head_copy(step, sl, h):
        rows = pl.ds((step % nb) * tm, tm)
        return pltpu.make_async_copy(stage_ref.at[sl, :, pl.ds(h * HEAD_DIM, HEAD_DIM)],
                                     of_hbm.at[step // nb, rows, h, :], sem_ref.at[sl])

    def wait_step(step, sl):
        for h in range(SB_HEADS):
            head_copy(step, sl, h).wait()

    @pl.when(i >= 2)
    def _():
        wait_step(i - 2, slot)

    acc = _dot(a_ref[...], w_ref[...].astype(BF16))
    ob_ref[...] = acc.astype(ob_ref.dtype)
    stage_ref[slot] = acc
    for h in range(SB_HEADS):
        head_copy(i, slot, h).start()

    @pl.when(i == nsteps - 1)
    def _():
        if nsteps >= 2:
            wait_step(i - 1, 1 - slot)
        wait_step(i, slot)


def _kv_proj_dma(a, w, col_blk, *, batch, seq, tm=512):
    m, k = a.shape
    tm = _tile(seq, tm)
    nb = seq // tm
    nsteps = m // tm
    blocks = _nbytes((tm, k), BF16) + _nbytes((tm, MAIN_WIDTH), BF16)
    return pl.pallas_call(
        functools.partial(_kv_dma_kernel, nb=nb, nsteps=nsteps),
        grid=(nsteps,),
        in_specs=[pl.BlockSpec((tm, k), lambda i: (i, 0)),
                  pl.BlockSpec((k, MAIN_WIDTH), lambda i: (0, col_blk), pipeline_mode=pl.Buffered(1))],
        out_specs=[pl.BlockSpec(memory_space=pl.ANY),
                   pl.BlockSpec((tm, MAIN_WIDTH), lambda i: (i, 0))],
        out_shape=[jax.ShapeDtypeStruct((batch, seq, SB_HEADS, HEAD_DIM), F32),
                   jax.ShapeDtypeStruct((m, MAIN_WIDTH), BF16)],
        scratch_shapes=[pltpu.VMEM((2, tm, MAIN_WIDTH), F32), pltpu.SemaphoreType.DMA((2,))],
        compiler_params=pltpu.CompilerParams(
            dimension_semantics=("arbitrary",),
            vmem_limit_bytes=_vmem_limit(
                blocks, scratch_bytes=_nbytes((k, MAIN_WIDTH), w.dtype) + _nbytes((2, tm, MAIN_WIDTH), F32),
                temp_bytes=2 * _nbytes((tm, MAIN_WIDTH), F32))),
        name="kv_proj_dma",
    )(a, w)


def _kv_proj(a, w, col_blk, *, batch, seq, tm=512):
    m, k = a.shape
    tm = _tile(seq, tm)
    nb = seq // tm
    sublane_padded_heads = -(-SB_HEADS // 8) * 8
    blocks = (_nbytes((tm, k), BF16) + _nbytes((tm, MAIN_WIDTH), BF16)
              + _nbytes((tm, sublane_padded_heads, HEAD_DIM), F32))
    return pl.pallas_call(
        _kv_kernel,
        grid=(m // tm,),
        in_specs=[pl.BlockSpec((tm, k), lambda i: (i, 0)),
                  pl.BlockSpec((k, MAIN_WIDTH), lambda i: (0, col_blk), pipeline_mode=pl.Buffered(1))],
        out_specs=[pl.BlockSpec((1, tm, SB_HEADS, HEAD_DIM), lambda i: (i // nb, i % nb, 0, 0)),
                   pl.BlockSpec((tm, MAIN_WIDTH), lambda i: (i, 0))],
        out_shape=[jax.ShapeDtypeStruct((batch, seq, SB_HEADS, HEAD_DIM), F32),
                   jax.ShapeDtypeStruct((m, MAIN_WIDTH), BF16)],
        compiler_params=pltpu.CompilerParams(
            dimension_semantics=("arbitrary",),
            vmem_limit_bytes=_vmem_limit(blocks, scratch_bytes=_nbytes((k, MAIN_WIDTH), w.dtype),
                                         temp_bytes=2 * _nbytes((tm, MAIN_WIDTH), F32))),
        name="kv_proj",
    )(a, w)


def _gla_tables(c):
    nlev = int(math.log2(c))
    assert 1 << nlev == c
    idx = np.arange(c)
    i, t = idx[:, None], idx[None, :]
    sums, pairs = [], []
    for lev in range(nlev):
        h = c >> (lev + 1)
        p = (i // (2 * h)) * 2 * h + h - 1
        second = (i % (2 * h)) >= h
        sums.append(np.where(second, (t > p) & (t <= i), (t > i) & (t <= p)))
        pairs.append((i // (2 * h) == t // (2 * h)) & second & ((t % (2 * h)) < h))
    sums.append(t <= i)
    sums.append(t > i)
    pairs.append(i == t)
    return (np.concatenate(sums, 0).astype(np.float32), np.stack(pairs, 0).astype(np.float32), nlev)


def _gla_kernel(v_ref, g_ref, q_ref, k_ref, lr_ref, wlr_ref, bg_ref, gn_ref, sum_ref, pair_ref, s0_ref,
                o_ref, sf_ref, st_ref, la_ref, og_ref, *, c, nchunk, nlev, nt):
    t = pl.program_id(1)

    @pl.when(t == 0)
    def _():
        pad = jnp.zeros((GLA_DKP - GLA_DK, GLA_DV), F32)
        for h in range(GLA_HEADS):
            st_ref[h] = jnp.concatenate([s0_ref[0, h], pad], axis=0).T

    zg = _dot(lr_ref[...], wlr_ref[...]) + bg_ref[...]
    la_ref[...] = (-_softplus(-zg) * (1.0 / GLA_TAU)).astype(BF16)

    def head(h, rows, w_all):
        kc = slice(h * GLA_DKP, (h + 1) * GLA_DKP)
        vc = slice(h * GLA_DV, (h + 1) * GLA_DV)
        w = w_all[:, kc]
        qb = q_ref[rows, kc]
        kb = k_ref[rows, kc]
        vb = v_ref[rows, vc]
        q = qb.astype(F32)
        k = kb.astype(F32)
        att = pair_ref[nlev] * _dot_nt(qb, kb)
        for lev in range(nlev):
            wl = w[lev * c:(lev + 1) * c]
            att = att + pair_ref[lev] * _dot_nt((q * wl).astype(BF16), (k * wl).astype(BF16))
        w_pre = w[nlev * c:(nlev + 1) * c]
        w_suf = w[(nlev + 1) * c:(nlev + 2) * c]
        st = st_ref[h]
        og_ref[rows, vc] = _dot(att.astype(BF16), vb) + _dot_nt((q * w_pre).astype(BF16), st.astype(BF16))
        st_ref[h] = st * w_pre[c - 1:c, :] + _dot_tn(vb, (k * w_suf).astype(BF16))

    def chunk(ci, carry):
        rows = pl.ds(pl.multiple_of(ci * c, c), c)
        w_all = jnp.exp(_dot(sum_ref[...], la_ref[rows, :]))
        for h in range(GLA_HEADS):
            head(h, rows, w_all)
        return carry

    lax.fori_loop(0, nchunk, chunk, 0, unroll=min(2, nchunk))

    for h in range(GLA_HEADS):
        vc = slice(h * GLA_DV, (h + 1) * GLA_DV)
        o = og_ref[:, vc]
        ms = jnp.mean(o * o, axis=-1, keepdims=True)
        on = o * lax.rsqrt(ms + RMS_EPS) * gn_ref[...]
        g = g_ref[:, vc].astype(F32)
        o_ref[:, vc] = (on * (g * _sigmoid(g))).astype(o_ref.dtype)

    @pl.when(t == nt - 1)
    def _():
        for h in range(GLA_HEADS):
            sf_ref[0, h] = st_ref[h].T[:GLA_DK, :]


def _gla(proj, w_lr2p, b_gatep, g_norm, s0, *, batch, seq):
    c = CHUNK if seq % CHUNK == 0 else seq
    tt = _tile(seq, 512)
    if tt % c:
        tt = c
    nt = seq // tt
    sums_np, pairs_np, nlev = _gla_tables(c)
    sums = jnp.asarray(sums_np, BF16)
    pairs = jnp.asarray(pairs_np, F32)
    nrow = sums_np.shape[0]
    kw = GLA_HEADS * GLA_DKP
    state_blk = (1, GLA_HEADS, GLA_DK, GLA_DV)
    state_vmem = (GLA_HEADS, GLA_DV, GLA_DKP)

    in_specs = [
        pl.BlockSpec((tt, MAIN_WIDTH), lambda b, t: (b * nt + t, A_OFF_V // MAIN_WIDTH)),
        pl.BlockSpec((tt, MAIN_WIDTH), lambda b, t: (b * nt + t, A_OFF_G // MAIN_WIDTH)),
        pl.BlockSpec((tt, kw), lambda b, t: (b * nt + t, A_OFF_Q // kw)),
        pl.BlockSpec((tt, kw), lambda b, t: (b * nt + t, A_OFF_K // kw)),
        pl.BlockSpec((tt, GLA_LRP), lambda b, t: (b * nt + t, A_OFF_LR // GLA_LRP)),
        pl.BlockSpec((GLA_LRP, kw), lambda b, t: (0, 0)),
        pl.BlockSpec((1, kw), lambda b, t: (0, 0)),
        pl.BlockSpec((1, GLA_DV), lambda b, t: (0, 0)),
        pl.BlockSpec((nrow, c), lambda b, t: (0, 0)),
        pl.BlockSpec((nlev + 1, c, c), lambda b, t: (0, 0, 0)),
        pl.BlockSpec(state_blk, lambda b, t: (b, 0, 0, 0)),
    ]
    blocks = (3 * _nbytes((tt, MAIN_WIDTH), BF16) + 2 * _nbytes((tt, kw), BF16) + _nbytes((tt, GLA_LRP), BF16)
              + _nbytes((GLA_LRP, kw), BF16) + 2 * _nbytes(state_blk, F32) + _nbytes((nrow, c), BF16)
              + _nbytes((nlev + 1, c, c), F32))
    return pl.pallas_call(
        functools.partial(_gla_kernel, c=c, nchunk=tt // c, nlev=nlev, nt=nt),
        grid=(batch, nt),
        in_specs=in_specs,
        out_specs=[pl.BlockSpec((tt, MAIN_WIDTH), lambda b, t: (b * nt + t, 0)),
                   pl.BlockSpec(state_blk, lambda b, t: (b, 0, 0, 0))],
        out_shape=[jax.ShapeDtypeStruct((batch * seq, MAIN_WIDTH), BF16),
                   jax.ShapeDtypeStruct((batch, GLA_HEADS, GLA_DK, GLA_DV), F32)],
        scratch_shapes=[pltpu.VMEM(state_vmem, F32), pltpu.VMEM((tt, kw), BF16),
                        pltpu.VMEM((tt, MAIN_WIDTH), F32)],
        compiler_params=pltpu.CompilerParams(
            dimension_semantics=("arbitrary", "arbitrary"),
            vmem_limit_bytes=_vmem_limit(
                blocks, temp_bytes=8 << 20,
                scratch_bytes=_nbytes(state_vmem, F32) + _nbytes((tt, kw), BF16) + _nbytes((tt, MAIN_WIDTH), F32))),
        name="gla_scan",
    )(proj, proj, proj, proj, proj, w_lr2p, b_gatep, g_norm, sums, pairs, s0)


def _sb_kernel(q_ref, k_ref, v_ref, u_ref, o_ref, acc_ref, carry_ref, *, tq, tk, q_off, nh):
    qi = pl.program_id(2)
    jd = (q_off + qi * tq) // tk
    q_pos = q_off + qi * tq + lax.broadcasted_iota(jnp.int32, (tq, tk), 0)
    ntile = tk // V7X_LANES

    def scores(h, j, reach):
        cols = slice(h * HEAD_DIM, (h + 1) * HEAD_DIM)
        z = _dot_nt(q_ref[:, cols], k_ref[pl.ds(pl.multiple_of(j * tk, tk), tk), cols])
        sp = jnp.maximum(z, 0.0) + jnp.log2(1.0 + jnp.exp2(-jnp.abs(z)))
        if reach is not None:
            sp = jnp.where(reach, sp, 0.0)
        return z, sp

    def suffix_sums(sps):
        stacked = jnp.concatenate([sp.astype(BF16) for sp in sps], axis=0)
        cs = _dot(stacked, u_ref[...])
        return [cs[n * tq:(n + 1) * tq] for n in range(len(sps))]

    def weighted(h, j, z, cs, carry, reach):
        cols = slice(h * HEAD_DIM, (h + 1) * HEAD_DIM)
        w = jnp.exp2(z - cs - jnp.concatenate([carry] * ntile, axis=1))
        if reach is not None:
            w = jnp.where(reach, w, 0.0)
        return _dot(w.astype(BF16), v_ref[pl.ds(pl.multiple_of(j * tk, tk), tk), cols])

    def live(carries):
        low = functools.reduce(jnp.minimum, carries)
        return jnp.min(low, axis=0, keepdims=True)[0, 0] < SB_DEAD_LOG2

    k_col = lax.broadcasted_iota(jnp.int32, (tq, tk), 1)
    diag_reach = (jd * tk + k_col) < q_pos
    has_prev = jnp.where(jd >= 1, 1.0, 0.0)
    j_prev = jnp.maximum(jd - 1, 0)
    zs, sps = [], []
    for h in range(nh):
        for j, reach in ((jd, diag_reach), (j_prev, None)):
            z, sp = scores(h, j, reach)
            zs.append(z)
            sps.append(sp)
    css = suffix_sums(sps)
    carries = []
    for h in range(nh):
        c0 = jnp.broadcast_to(jnp.sum(sps[2 * h], axis=1, keepdims=True), (tq, V7X_LANES))
        c1 = c0 + jnp.sum(sps[2 * h + 1], axis=1, keepdims=True) * has_prev
        a0 = weighted(h, jd, zs[2 * h], css[2 * h], jnp.zeros((tq, V7X_LANES), F32), diag_reach)
        a1 = weighted(h, j_prev, zs[2 * h + 1], css[2 * h + 1], c0, None)
        acc_ref[h] = a0 + a1 * has_prev
        carry_ref[h] = c1
        carries.append(c1)

    def cond(state):
        j, alive = state
        return jnp.logical_and(j >= 0, alive)

    def body(state):
        j, _ = state
        zs, sps = zip(*[scores(h, j, None) for h in range(nh)])
        css = suffix_sums(sps)
        new = []
        for h in range(nh):
            carry = carry_ref[h]
            acc_ref[h] += weighted(h, j, zs[h], css[h], carry, None)
            carry_ref[h] = carry + jnp.sum(sps[h], axis=1, keepdims=True)
            new.append(carry_ref[h])
        return j - 1, live(new)

    lax.while_loop(cond, body, (jd - 2, live(carries)))
    for h in range(nh):
        o_ref[:, h * HEAD_DIM:(h + 1) * HEAD_DIM] = acc_ref[h].astype(o_ref.dtype)


def _sb_attention(q_arr, q_col0, k_arr, v_arr, *, batch, seq_q, seq_k, q_off):
    tq = min(SB_TILE, seq_q)
    tk = SB_TILE
    nh = SB_HEADS_PER_STEP
    width = nh * HEAD_DIM
    assert seq_q % tq == 0 and seq_k % tk == 0 and q_off % tk == 0 and tq <= tk
    assert q_col0 % width == 0 and SB_HEADS % nh == 0
    nq = seq_q // tq
    u = jnp.asarray(np.tril(np.ones((tk, tk), np.float32)), BF16)
    blocks = 2 * _nbytes((tq, width), BF16) + 2 * _nbytes((seq_k, width), BF16) + _nbytes((tk, tk), BF16)
    return pl.pallas_call(
        functools.partial(_sb_kernel, tq=tq, tk=tk, q_off=q_off, nh=nh),
        grid=(batch, SB_HEADS // nh, nq),
        in_specs=[pl.BlockSpec((tq, width), lambda b, h, i: (b * nq + i, q_col0 // width + h)),
                  pl.BlockSpec((seq_k, width), lambda b, h, i: (b, h)),
                  pl.BlockSpec((seq_k, width), lambda b, h, i: (b, h)),
                  pl.BlockSpec((tk, tk), lambda b, h, i: (0, 0))],
        out_specs=pl.BlockSpec((tq, width), lambda b, h, i: (b * nq + i, h)),
        out_shape=jax.ShapeDtypeStruct((batch * seq_q, MAIN_WIDTH), BF16),
        scratch_shapes=[pltpu.VMEM((nh, tq, HEAD_DIM), F32), pltpu.VMEM((nh, tq, V7X_LANES), F32)],
        compiler_params=pltpu.CompilerParams(
            dimension_semantics=("arbitrary", "arbitrary", "arbitrary"),
            vmem_limit_bytes=_vmem_limit(blocks, temp_bytes=12 * nh * _nbytes((tq, tk), F32))),
        name="stick_breaking",
    )(q_arr, k_arr, v_arr, u)


def _cache_join_kernel(ck_ref, cv_ref, nk_ref, nv_ref, ok_ref, ov_ref, *, past):
    for c_ref, n_ref, o_ref in ((ck_ref, nk_ref, ok_ref), (cv_ref, nv_ref, ov_ref)):
        o_ref[0:past, :] = c_ref[0, 0].astype(o_ref.dtype)
        zeros = jnp.zeros((o_ref.shape[0] - past - n_ref.shape[0], o_ref.shape[1]), o_ref.dtype)
        o_ref[past:, :] = jnp.concatenate([n_ref[...], zeros], axis=0)


def _cache_join(cache_k, cache_v, new_k, new_v):
    bs, past, nheads, hd = cache_k.shape
    new_len = new_k.shape[0] // bs
    lk_pad = past + SB_TILE
    assert past % SB_TILE == 0 and new_len <= SB_TILE and new_len % 16 == 0
    cache_spec = pl.BlockSpec((1, 1, past, hd), lambda b, h: (b, h, 0, 0))
    new_spec = pl.BlockSpec((new_len, hd), lambda b, h: (b, h))
    out_spec = pl.BlockSpec((lk_pad, hd), lambda b, h: (b, h))
    out_sds = jax.ShapeDtypeStruct((bs * lk_pad, nheads * hd), BF16)
    blocks = 2 * (_nbytes((past, hd), F32) + _nbytes((new_len, hd), BF16) + _nbytes((lk_pad, hd), BF16))
    return pl.pallas_call(
        functools.partial(_cache_join_kernel, past=past),
        grid=(bs, nheads),
        in_specs=[cache_spec, cache_spec, new_spec, new_spec],
        out_specs=[out_spec, out_spec],
        out_shape=[out_sds, out_sds],
        compiler_params=pltpu.CompilerParams(
            dimension_semantics=("arbitrary", "arbitrary"),
            vmem_limit_bytes=_vmem_limit(blocks)),
        name="cache_join",
    )(jnp.swapaxes(cache_k, 1, 2), jnp.swapaxes(cache_v, 1, 2), new_k, new_v)


def _mem_kernel(qm_ref, mk_ref, mv_ref, o_ref):
    for h in range(MEM_HEADS):
        cols = slice(h * HEAD_DIM, (h + 1) * HEAD_DIM)
        s = _dot_nt(qm_ref[:, cols], mk_ref[0, :, cols])
        e = jnp.exp(s - jnp.max(s, axis=-1, keepdims=True))
        den = jnp.sum(e, axis=-1, keepdims=True)
        o = _dot(e.astype(BF16), mv_ref[0, :, cols]) / den
        o_ref[:, cols] = o.astype(o_ref.dtype)


def _mem_attention(q_arr, q_col0, mk, mv, *, batch, seq):
    tt = _tile(seq, 512)
    nt = seq // tt
    mlen = mk.shape[1]
    assert q_col0 % MEM_WIDTH == 0
    blocks = 2 * _nbytes((tt, MEM_WIDTH), BF16) + 2 * _nbytes((mlen, MEM_WIDTH), BF16)
    return pl.pallas_call(
        _mem_kernel,
        grid=(batch, nt),
        in_specs=[pl.BlockSpec((tt, MEM_WIDTH), lambda b, t: (b * nt + t, q_col0 // MEM_WIDTH)),
                  pl.BlockSpec((1, mlen, MEM_WIDTH), lambda b, t: (b, 0, 0)),
                  pl.BlockSpec((1, mlen, MEM_WIDTH), lambda b, t: (b, 0, 0))],
        out_specs=pl.BlockSpec((tt, MEM_WIDTH), lambda b, t: (b * nt + t, 0)),
        out_shape=jax.ShapeDtypeStruct((batch * seq, MEM_WIDTH), BF16),
        compiler_params=pltpu.CompilerParams(
            dimension_semantics=("arbitrary", "arbitrary"),
            vmem_limit_bytes=_vmem_limit(blocks, temp_bytes=6 * _nbytes((tt, mlen), F32))),
        name="mem_attention",
    )(q_arr, mk, mv)


def _pad_heads(w, width, padded):
    lead = w.shape[:-1]
    w = w.reshape(lead + (GLA_HEADS, width))
    w = jnp.pad(w, [(0, 0)] * len(lead) + [(0, 0), (0, padded - width)])
    return w.reshape(lead + (GLA_HEADS * padded,))


def _layout_in_a(w_in):
    q = w_in[:, :GLA_KW]
    k = w_in[:, GLA_KW:2 * GLA_KW]
    v = w_in[:, 2 * GLA_KW:2 * GLA_KW + MAIN_WIDTH]
    g = w_in[:, 2 * GLA_KW + MAIN_WIDTH:2 * GLA_KW + 2 * MAIN_WIDTH]
    lr = w_in[:, 2 * GLA_KW + 2 * MAIN_WIDTH:2 * GLA_KW + 2 * MAIN_WIDTH + GLA_LOWRANK]
    qm = w_in[:, 2 * GLA_KW + 2 * MAIN_WIDTH + GLA_LOWRANK:]
    lr = jnp.pad(lr, ((0, 0), (0, GLA_LRP - GLA_LOWRANK)))
    w = jnp.concatenate([v, g, _pad_heads(q, GLA_DK, GLA_DKP), _pad_heads(k, GLA_DK, GLA_DKP), qm, lr], axis=1)
    scale = np.ones((1, A_WIDTH), np.float32)
    scale[:, A_OFF_Q:A_OFF_K] = GLA_DK ** -0.5
    scale[:, A_OFF_QM:A_OFF_LR] = MEM_SCALE
    return w.astype(BF16), jnp.asarray(scale)


def _scale_in_b():
    scale = np.ones((1, D_MODEL), np.float32)
    scale[:, :MAIN_WIDTH] = SB_SCALE * LOG2_E
    scale[:, MAIN_WIDTH:] = MEM_SCALE
    return jnp.asarray(scale)


def _finish_layer(x, o_main, o_mem, lw, shared):
    l = lw["layer"]
    x1, x1b = _matmul_residual_ln([o_main, o_mem], shared["w_o"], l, x, lw["ln1_g"], lw["ln1_b"], tm=512,
                                  nsplit=4)
    hid = _ffn_in(x1b, shared["w_ffn_in"], l)
    return _matmul_residual_ln([hid], shared["w_ffn_out"], l, x1, lw["ln2_g"], lw["ln2_b"], tm=256)


def _layer_a(x, xb, lw, shared, mk, mv, s0, *, batch, seq):
    proj, = _matmul(xb, lw["w_in"], out_dtypes=(BF16,), col_scale=lw["in_scale"], tm=1024, tn=1920)
    o_main, s_final = _gla(proj, lw["w_lr2p"], lw["b_gatep"], lw["g_norm"], s0, batch=batch, seq=seq)
    o_mem = _mem_attention(proj, A_OFF_QM, mk, mv, batch=batch, seq=seq)
    x, xb = _finish_layer(x, o_main, o_mem, lw, shared)
    return x, xb, s_final


def _layer_b(x, xb, lw, shared, mk, mv, kb, vb, *, batch, seq, seq_k, q_off):
    proj, = _matmul(xb, shared["w_in_b"], layer=lw["layer"] - N_A, out_dtypes=(BF16,),
                    col_scale=shared["in_b_scale"], tm=1024, tn=1024)
    o_main = _sb_attention(proj, 0, kb, vb, batch=batch, seq_q=seq, seq_k=seq_k, q_off=q_off)
    o_mem = _mem_attention(proj, MAIN_WIDTH, mk, mv, batch=batch, seq=seq)
    return _finish_layer(x, o_main, o_mem, lw, shared)


def kernel(x_prompt, x_sample, mem_prompt, state_gla, cache_sb_k, cache_sb_v, cache_mem_k, cache_mem_v,
           w_in_a, w_gate_lr, b_gate, gla_norm_g, w_in_b, w_kv_shared, w_mem_kv, w_o, ln1_g, ln1_b,
           ln2_g, ln2_b, w_ffn_in, w_ffn_out):
    bp, lp, _ = x_prompt.shape
    bs, ls, _ = x_sample.shape
    mlen = mem_prompt.shape[1]

    shared = {
        "w_o": w_o.astype(BF16),
        "w_ffn_in": w_ffn_in,
        "w_ffn_out": w_ffn_out.astype(BF16),
        "w_in_b": w_in_b,
        "in_b_scale": _scale_in_b(),
    }
    layers = []
    for l in range(DEPTH):
        lw = {
            "layer": l,
            "ln1_g": ln1_g[l][None, :], "ln1_b": ln1_b[l][None, :],
            "ln2_g": ln2_g[l][None, :], "ln2_b": ln2_b[l][None, :],
        }
        if l < N_A:
            lw["w_in"], lw["in_scale"] = _layout_in_a(w_in_a[l])
            lr2 = _pad_heads(w_gate_lr[l], GLA_DK, GLA_DKP)
            lw["w_lr2p"] = jnp.pad(lr2, ((0, GLA_LRP - GLA_LOWRANK), (0, 0))).astype(BF16)
            lw["b_gatep"] = _pad_heads(b_gate[l][None, :], GLA_DK, GLA_DKP)
            lw["g_norm"] = gla_norm_g[l][None, :]
        layers.append(lw)
    w_kv = w_kv_shared
    w_memkv = w_mem_kv

    xp = x_prompt.reshape(bp * lp, D_MODEL)
    xs = x_sample.reshape(bs * ls, D_MODEL)
    xpb, xsb = xp, xs
    memb = mem_prompt.reshape(bp * mlen, D_MODEL).astype(BF16)
    cmk = cache_mem_k.reshape(DEPTH, bs, mlen, MEM_WIDTH).astype(BF16)
    cmv = cache_mem_v.reshape(DEPTH, bs, mlen, MEM_WIDTH).astype(BF16)

    s_zero = jnp.zeros((bp, GLA_HEADS, GLA_DK, GLA_DV), F32)
    gla_p, gla_s, mk_list, mv_list = [], [], [], []
    for l in range(DEPTH):
        lw = layers[l]
        mkv, = _matmul(memb, w_memkv, layer=l, out_dtypes=(F32,), tm=1024, tn=1024)
        mk_p, mv_p = mkv[:, :MEM_WIDTH], mkv[:, MEM_WIDTH:]
        mk_list.append(mk_p.reshape(bp, mlen, MEM_HEADS, HEAD_DIM))
        mv_list.append(mv_p.reshape(bp, mlen, MEM_HEADS, HEAD_DIM))
        mk_pb = mk_p.astype(BF16).reshape(bp, mlen, MEM_WIDTH)
        mv_pb = mv_p.astype(BF16).reshape(bp, mlen, MEM_WIDTH)
        if l < N_A:
            xp, xpb, sp = _layer_a(xp, xpb, lw, shared, mk_pb, mv_pb, s_zero, batch=bp, seq=lp)
            xs, xsb, ss = _layer_a(xs, xsb, lw, shared, cmk[l], cmv[l], state_gla[l], batch=bs, seq=ls)
            gla_p.append(sp)
            gla_s.append(ss)
        else:
            if l == N_A:
                kp_f, kp_b = _kv_proj_dma(xpb, w_kv, 0, batch=bp, seq=lp)
                vp_f, vp_b = _kv_proj_dma(xpb, w_kv, 1, batch=bp, seq=lp)
                ks_f, ks_b = _kv_proj(xsb, w_kv, 0, batch=bs, seq=ls)
                vs_f, vs_b = _kv_proj(xsb, w_kv, 1, batch=bs, seq=ls)
                lk_pad = PAST_LEN + SB_TILE
                k_all_b, v_all_b = _cache_join(cache_sb_k, cache_sb_v, ks_b, vs_b)
            xp, xpb = _layer_b(xp, xpb, lw, shared, mk_pb, mv_pb, kp_b, vp_b, batch=bp, seq=lp, seq_k=lp,
                               q_off=0)
            xs, xsb = _layer_b(xs, xsb, lw, shared, cmk[l], cmv[l], k_all_b, v_all_b, batch=bs, seq=ls,
                               seq_k=lk_pad, q_off=PAST_LEN)

    return (xp.reshape(bp, lp, D_MODEL), xs.reshape(bs, ls, D_MODEL),
            jnp.stack(gla_p, 0), jnp.stack(gla_s, 0), kp_f, vp_f, ks_f, vs_f,
            jnp.stack(mk_list, 0), jnp.stack(mv_list, 0))
```
